```python
import math
import jax, jax.numpy as jnp
from jax import lax
import numpy as np

D_MODEL = 1024
BATCH = 8
SEQ = 2048
DEPTH = 4

N_MIXERS = 3
EPS = 1e-6
SB_HEAD_DIM = 64
SB_HEADS = D_MODEL // SB_HEAD_DIM
Q_BLOCK = 128
CHUNK = 128
SG_WIDTH = D_MODEL
SG_GROUPS = 8
SG_HEAD_DIM = SG_WIDTH // SG_GROUPS
SSM_GROUP = 16
SSM_GROUPS = D_MODEL // SSM_GROUP
SSM_STATE = 64
DT_MIN = 1e-3
DT_MAX = 1e-1
D_FF = 2816
CONV_K = 3
N_A = (DEPTH + 2) // 3
N_B = (DEPTH + 1) // 3
N_C = DEPTH // 3

kernel_name = "hybrid_stickbreak_gmlp_s5_trunk"


def rmsnorm(x, g):
    xf = x.astype(jnp.float32)
    y = xf * lax.rsqrt(jnp.mean(xf * xf, axis=-1, keepdims=True) + EPS) * g.astype(jnp.float32)
    return y.astype(x.dtype)


def stick_breaking_attention(xn, w_qkv, w_o):
    B, L, _ = xn.shape
    qkv = (xn @ w_qkv).reshape(B, L, 3, SB_HEADS, SB_HEAD_DIM).astype(jnp.float32)
    q = qkv[:, :, 0] * (SB_HEAD_DIM ** -0.5)
    k = qkv[:, :, 1]
    v = qkv[:, :, 2]
    blocks = []
    for i in range(L // Q_BLOCK):
        t0 = i * Q_BLOCK
        t1 = t0 + Q_BLOCK
        qb, kb, vb = q[:, t0:t1], k[:, :t1], v[:, :t1]
        z = jnp.einsum('bthd,bshd->bhts', qb, kb)
        t_pos = t0 + jnp.arange(Q_BLOCK)
        s_pos = jnp.arange(t1)
        mask = s_pos[None, :] < t_pos[:, None]
        log_1m = jnp.where(mask, jax.nn.log_sigmoid(-z), 0.0)
        log_w = jax.nn.log_sigmoid(z) + lax.cumsum(log_1m, axis=3, reverse=True) - log_1m
        w = jnp.where(mask, jnp.exp(log_w), 0.0)
        blocks.append(jnp.einsum('bhts,bshd->bthd', w, vb))
    o = jnp.concatenate(blocks, axis=1).reshape(B, L, D_MODEL).astype(xn.dtype)
    return o @ w_o


def chunked_spatial_gating(xn, w_in, v_norm_g, w_s, b_s, w_o):
    B, L, _ = xn.shape
    h = jax.nn.gelu(xn @ w_in)
    u, v = jnp.split(h, 2, axis=-1)
    v = rmsnorm(v, v_norm_g)
    v = v.reshape(B, L // CHUNK, CHUNK, SG_GROUPS, SG_HEAD_DIM)
    w_causal = jnp.tril(w_s)
    sv = jnp.einsum('gts,bcsgd->bctgd', w_causal, v) + b_s.T[:, :, None]
    return (u * sv.reshape(B, L, SG_WIDTH)) @ w_o


def _complex_affine_combine(e1, e2):
    a1r, a1i, b1r, b1i = e1
    a2r, a2i, b2r, b2i = e2
    ar = a2r * a1r - a2i * a1i
    ai = a2r * a1i + a2i * a1r
    br = a2r * b1r - a2i * b1i + b2r
    bi = a2r * b1i + a2i * b1r + b2i
    return (ar, ai, br, bi)


def s5_layer(xn, w_in, lam_re, lam_im, log_dt, b_re, b_im, c_re, c_im, d_skip, w_glu):
    B, L, _ = xn.shape
    u = (xn @ w_in).astype(jnp.float32)
    ug = u.reshape(B, L, SSM_GROUPS, SSM_GROUP)
    lr = jnp.minimum(lam_re.astype(jnp.float32), -1e-4)
    li = lam_im.astype(jnp.float32)
    dt = jnp.exp(log_dt.astype(jnp.float32))[:, None]
    mag = jnp.exp(dt * lr)
    ar = mag * jnp.cos(dt * li)
    ai = mag * jnp.sin(dt * li)
    den = lr * lr + li * li
    coef_re = ((ar - 1.0) * lr + ai * li) / den
    coef_im = (ai * lr - (ar - 1.0) * li) / den
    br32, bi32 = b_re.astype(jnp.float32), b_im.astype(jnp.float32)
    bbar_re = coef_re[..., None] * br32 - coef_im[..., None] * bi32
    bbar_im = coef_re[..., None] * bi32 + coef_im[..., None] * br32
    bu_re = jnp.einsum('gph,blgh->blgp', bbar_re, ug)
    bu_im = jnp.einsum('gph,blgh->blgp', bbar_im, ug)
    a_re = jnp.broadcast_to(ar, (1, L, SSM_GROUPS, SSM_STATE))
    a_im = jnp.broadcast_to(ai, (1, L, SSM_GROUPS, SSM_STATE))
    _, _, xr, xi = lax.associative_scan(_complex_affine_combine, (a_re, a_im, bu_re, bu_im), axis=1)
    y = (jnp.einsum('ghp,blgp->blgh', c_re.astype(jnp.float32), xr)
         - jnp.einsum('ghp,blgp->blgh', c_im.astype(jnp.float32), xi))
    y = y.reshape(B, L, D_MODEL) + d_skip.astype(jnp.float32) * u
    hg = jax.nn.gelu(y).astype(xn.dtype) @ w_glu
    a, g = jnp.split(hg, 2, axis=-1)
    return a * jax.nn.sigmoid(g)


def causal_depthwise_conv(h, w, b):
    L = h.shape[1]
    hp = jnp.pad(h, ((0, 0), (CONV_K - 1, 0), (0, 0)))
    y = hp[:, 0:L] * w[0]
    for kk in range(1, CONV_K):
        y = y + hp[:, kk:kk + L] * w[kk]
    return y + b


def conv_gated_ffn(xn, w_up, conv_w, conv_b, w_down):
    h = causal_depthwise_conv(xn @ w_up, conv_w, conv_b)
    a, g = jnp.split(h, 2, axis=-1)
    return (jax.nn.silu(g) * a) @ w_down


def setup_inputs(seed: int = 0) -> dict:
    key = jax.random.key(seed)
    ks = jax.random.split(key, 32)
    f32 = jnp.float32

    def nrm(k, shape, scale):
        return jax.random.normal(k, shape, f32) * scale

    x = nrm(ks[0], (BATCH, SEQ, D_MODEL), 1.0)
    norm_g = 1.0 + nrm(ks[1], (DEPTH, 2, D_MODEL), 0.05)
    final_norm_g = 1.0 + nrm(ks[2], (D_MODEL,), 0.05)
    sb_w_qkv = nrm(ks[3], (N_A, D_MODEL, 3 * D_MODEL), D_MODEL ** -0.5)
    sb_w_o = nrm(ks[4], (N_A, D_MODEL, D_MODEL), D_MODEL ** -0.5)
    sg_w_in = nrm(ks[5], (N_B, D_MODEL, 2 * SG_WIDTH), D_MODEL ** -0.5)
    sg_norm_g = 1.0 + nrm(ks[6], (N_B, SG_WIDTH), 0.05)
    sg_w_s = nrm(ks[7], (N_B, SG_GROUPS, CHUNK, CHUNK), CHUNK ** -0.5)
    sg_b = 1.0 + nrm(ks[8], (N_B, SG_GROUPS, CHUNK), 0.1)
    sg_w_o = nrm(ks[9], (N_B, SG_WIDTH, D_MODEL), SG_WIDTH ** -0.5)
    ssm_w_in = nrm(ks[10], (N_C, D_MODEL, D_MODEL), D_MODEL ** -0.5)
    ssm_lam_re = -0.5 + nrm(ks[11], (N_C, SSM_GROUPS, SSM_STATE), 0.01)
    ssm_lam_im = (jnp.pi * jnp.arange(SSM_STATE, dtype=f32)
                  + nrm(ks[12], (N_C, SSM_GROUPS, SSM_STATE), 0.01))
    ssm_log_dt = jax.random.uniform(ks[13], (N_C, SSM_GROUPS), f32,
                                    minval=math.log(DT_MIN), maxval=math.log(DT_MAX))
    ssm_b_re = nrm(ks[14], (N_C, SSM_GROUPS, SSM_STATE, SSM_GROUP), (2 * SSM_GROUP) ** -0.5)
    ssm_b_im = nrm(ks[15], (N_C, SSM_GROUPS, SSM_STATE, SSM_GROUP), (2 * SSM_GROUP) ** -0.5)
    ssm_c_re = nrm(ks[16], (N_C, SSM_GROUPS, SSM_GROUP, SSM_STATE), SSM_STATE ** -0.5)
    ssm_c_im = nrm(ks[17], (N_C, SSM_GROUPS, SSM_GROUP, SSM_STATE), SSM_STATE ** -0.5)
    ssm_d = nrm(ks[18], (N_C, D_MODEL), 1.0)
    ssm_w_glu = nrm(ks[19], (N_C, D_MODEL, 2 * D_MODEL), D_MODEL ** -0.5)
    ffn_w_up = nrm(ks[20], (DEPTH, D_MODEL, 2 * D_FF), D_MODEL ** -0.5)
    ffn_conv_w = nrm(ks[21], (DEPTH, CONV_K, 2 * D_FF), CONV_K ** -0.5)
    ffn_conv_b = nrm(ks[22], (DEPTH, 2 * D_FF), 0.01)
    ffn_w_down = nrm(ks[23], (DEPTH, D_FF, D_MODEL), D_FF ** -0.5)
    return {
        "x": x, "norm_g": norm_g, "final_norm_g": final_norm_g,
        "sb_w_qkv": sb_w_qkv, "sb_w_o": sb_w_o,
        "sg_w_in": sg_w_in, "sg_norm_g": sg_norm_g, "sg_w_s": sg_w_s, "sg_b": sg_b, "sg_w_o": sg_w_o,
        "ssm_w_in": ssm_w_in, "ssm_lam_re": ssm_lam_re, "ssm_lam_im": ssm_lam_im,
        "ssm_log_dt": ssm_log_dt, "ssm_b_re": ssm_b_re, "ssm_b_im": ssm_b_im,
        "ssm_c_re": ssm_c_re, "ssm_c_im": ssm_c_im, "ssm_d": ssm_d, "ssm_w_glu": ssm_w_glu,
        "ffn_w_up": ffn_w_up, "ffn_conv_w": ffn_conv_w, "ffn_conv_b": ffn_conv_b,
        "ffn_w_down": ffn_w_down,
    }


def reference(x, norm_g, final_norm_g, sb_w_qkv, sb_w_o, sg_w_in, sg_norm_g, sg_w_s, sg_b, sg_w_o,
              ssm_w_in, ssm_lam_re, ssm_lam_im, ssm_log_dt, ssm_b_re, ssm_b_im, ssm_c_re, ssm_c_im,
              ssm_d, ssm_w_glu, ffn_w_up, ffn_conv_w, ffn_conv_b, ffn_w_down):
    for i in range(DEPTH):
        mixer = i % N_MIXERS
        j = i // N_MIXERS
        xn = rmsnorm(x, norm_g[i, 0])
        if mixer == 0:
            m = stick_breaking_attention(xn, sb_w_qkv[j], sb_w_o[j])
        elif mixer == 1:
            m = chunked_spatial_gating(xn, sg_w_in[j], sg_norm_g[j], sg_w_s[j], sg_b[j], sg_w_o[j])
        else:
            m = s5_layer(xn, ssm_w_in[j], ssm_lam_re[j], ssm_lam_im[j], ssm_log_dt[j],
                         ssm_b_re[j], ssm_b_im[j], ssm_c_re[j], ssm_c_im[j], ssm_d[j], ssm_w_glu[j])
        x = x + m.astype(x.dtype)
        f = conv_gated_ffn(rmsnorm(x, norm_g[i, 1]), ffn_w_up[i], ffn_conv_w[i], ffn_conv_b[i],
                           ffn_w_down[i])
        x = x + f.astype(x.dtype)
    return rmsnorm(x, final_norm_g)
```

```python
import functools
import math

import jax
import jax.numpy as jnp
from jax import lax
from jax.experimental import pallas as pl
from jax.experimental.pallas import tpu as pltpu

F32 = jnp.float32
BF16 = jnp.bfloat16

D_MODEL = 1024
EPS = 1e-6
SB_HEAD_DIM = 64
SB_PAIR = 2 * SB_HEAD_DIM
SB_PAIRS = D_MODEL // SB_PAIR
Q_BLOCK = 128
CHUNK = 128
SG_GROUPS = 8
SG_HEAD_DIM = D_MODEL // SG_GROUPS
SSM_GROUP = 16
SSM_GROUPS = 64
SSM_STATE = 64
SSM_SLABS = 8
SSM_SLAB_GROUPS = SSM_GROUPS // SSM_SLABS
SSM_SLAB_STATES = SSM_SLAB_GROUPS * SSM_STATE
D_FF = 2816
CONV_K = 3
DEPTH = 4
N_MIXERS = 3

V7X_VMEM_LIMIT_BYTES = 56 * 1024 * 1024
SUBLANES = 8
LANES = 128

FFN_ROWS = 512
FFN_FC = 256
QKV_ROWS = 512
SG_ROWS = 256
SSM_STEPS = 64


def _rms(x, g):
    return x * lax.rsqrt(jnp.mean(x * x, axis=-1, keepdims=True) + EPS) * g


def _sigmoid(x):
    return 1.0 / (1.0 + jnp.exp(-x))


def _gelu(x):
    c = math.sqrt(2.0 / math.pi)
    return 0.5 * x * (1.0 + jnp.tanh(c * (x + 0.044715 * (x * x * x))))


def _resident(shape):
    nd = len(shape)
    return pl.BlockSpec(shape, lambda *_: (0,) * nd, pipeline_mode=pl.Buffered(1))


def _params(n_axes):
    return pltpu.CompilerParams(
        dimension_semantics=("arbitrary",) * n_axes,
        vmem_limit_bytes=V7X_VMEM_LIMIT_BYTES,
    )


def _ffn_kernel(*refs, glu, final, tl, fc, n_chunks):
    if final:
        (x_ref, m_ref, wm_ref, g_ref, wa_ref, wg_ref, cwa_ref, cwg_ref, cba_ref, cbg_ref, wd_ref,
         fg_ref, o_ref, xn_s, hsa, hsg, cara, carg, act) = refs
    else:
        (x_ref, m_ref, wm_ref, g_ref, wa_ref, wg_ref, cwa_ref, cwg_ref, cba_ref, cbg_ref, wd_ref,
         o_ref, xn_s, hsa, hsg, cara, carg, act) = refs
        fg_ref = None

    @pl.when(pl.program_id(1) == 0)
    def _():
        cara[...] = jnp.zeros_like(cara)
        carg[...] = jnp.zeros_like(carg)

    mm = jnp.dot(m_ref[...], wm_ref[...], preferred_element_type=F32)
    if glu:
        mix = mm[:, :D_MODEL] * _sigmoid(mm[:, D_MODEL:])
    else:
        mix = mm
    x1 = x_ref[...] + mix
    o_ref[...] = x1
    xn_s[...] = _rms(x1, g_ref[...]).astype(BF16)

    def conv(hs, car, cw_ref, cb_ref, h, cols):
        hs[0:SUBLANES, :] = car[:, cols]
        hs[SUBLANES:SUBLANES + tl, :] = h
        car[:, cols] = h[tl - SUBLANES:tl, :]
        w = cw_ref[:, cols]
        y = (w[2:3, :] * hs[SUBLANES:SUBLANES + tl, :]
             + w[1:2, :] * hs[SUBLANES - 1:SUBLANES - 1 + tl, :]
             + w[0:1, :] * hs[SUBLANES - 2:SUBLANES - 2 + tl, :])
        return y + cb_ref[:, cols]

    for c in range(n_chunks):
        cols = slice(c * fc, (c + 1) * fc)
        xn = xn_s[...]
        ha = jnp.dot(xn, wa_ref[:, cols], preferred_element_type=F32)
        hg = jnp.dot(xn, wg_ref[:, cols], preferred_element_type=F32)
        ya = conv(hsa, cara, cwa_ref, cba_ref, ha, cols)
        yg = conv(hsg, carg, cwg_ref, cbg_ref, hg, cols)
        act[:, cols] = (yg * _sigmoid(yg) * ya).astype(BF16)

    out = o_ref[...] + jnp.dot(act[...], wd_ref[...], preferred_element_type=F32)
    if final:
        out = _rms(out, fg_ref[...])
    o_ref[...] = out


def _ffn_call(x, m, wm, g, wa, wg, cwa, cwg, cba, cbg, wd, fg, *, glu, tl=FFN_ROWS, fc=FFN_FC):
    B, L, D = x.shape
    F = wa.shape[1]
    dm = m.shape[2]
    final = fg is not None
    n_chunks = F // fc
    row = lambda b, i: (b, i, 0)
    in_specs = [
        pl.BlockSpec((None, tl, D), row),
        pl.BlockSpec((None, tl, dm), row),
        _resident(wm.shape), _resident(g.shape), _resident(wa.shape), _resident(wg.shape),
        _resident(cwa.shape), _resident(cwg.shape), _resident(cba.shape), _resident(cbg.shape),
        _resident(wd.shape),
    ]
    args = [x, m, wm, g, wa, wg, cwa, cwg, cba, cbg, wd]
    if final:
        in_specs.append(_resident(fg.shape))
        args.append(fg)
    return pl.pallas_call(
        functools.partial(_ffn_kernel, glu=glu, final=final, tl=tl, fc=fc, n_chunks=n_chunks),
        grid=(B, L // tl),
        in_specs=in_specs,
        out_specs=pl.BlockSpec((None, tl, D), row),
        out_shape=jax.ShapeDtypeStruct((B, L, D), F32),
        scratch_shapes=[
            pltpu.VMEM((tl, D), BF16),
            pltpu.VMEM((tl + SUBLANES, fc), F32),
            pltpu.VMEM((tl + SUBLANES, fc), F32),
            pltpu.VMEM((SUBLANES, F), F32),
            pltpu.VMEM((SUBLANES, F), F32),
            pltpu.VMEM((tl, F), BF16),
        ],
        compiler_params=_params(2),
        name="ffn",
    )(*args)


def _qkv_kernel(x_ref, g_ref, wq_ref, wkt_ref, wv_ref, q_ref, kt_ref, v_ref, *, tl):
    xn = _rms(x_ref[...], g_ref[...]).astype(BF16)
    q = jnp.dot(xn, wq_ref[...], preferred_element_type=F32) * (SB_HEAD_DIM ** -0.5)
    q_ref[...] = q.astype(BF16)
    v_ref[...] = jnp.dot(xn, wv_ref[...], preferred_element_type=F32).astype(BF16)
    kt = lax.dot_general(wkt_ref[...], xn, (((1,), (1,)), ((), ())), preferred_element_type=F32)
    kt = kt.astype(BF16)
    for t in range(tl // Q_BLOCK):
        kt_ref[t] = kt[:, t * Q_BLOCK:(t + 1) * Q_BLOCK]


def _qkv_call(x, g, wq, wkt, wv, *, tl=QKV_ROWS):
    B, L, D = x.shape
    nk = tl // Q_BLOCK
    row = lambda b, i: (b, i, 0)
    return pl.pallas_call(
        functools.partial(_qkv_kernel, tl=tl),
        grid=(B, L // tl),
        in_specs=[pl.BlockSpec((None, tl, D), row), _resident(g.shape), _resident(wq.shape),
                  _resident(wkt.shape), _resident(wv.shape)],
        out_specs=[pl.BlockSpec((None, tl, D), row),
                   pl.BlockSpec((None, nk, D, Q_BLOCK), lambda b, i: (b, i, 0, 0)),
                   pl.BlockSpec((None, tl, D), row)],
        out_shape=[jax.ShapeDtypeStruct((B, L, D), BF16),
                   jax.ShapeDtypeStruct((B, L // Q_BLOCK, D, Q_BLOCK), BF16),
                   jax.ShapeDtypeStruct((B, L, D), BF16)],
        compiler_params=_params(2),
        name="sb_qkv",
    )(x, g, wq, wkt, wv)


def _sb_kernel(q_ref, kt_ref, v_ref, o_ref, acc, csum):
    i = pl.program_id(1)
    W = 2 * Q_BLOCK
    r = lax.broadcasted_iota(jnp.int32, (W, W), 0)
    c = lax.broadcasted_iota(jnp.int32, (W, W), 1)
    same = (r < Q_BLOCK) == (c < Q_BLOCK)
    tri = jnp.where(same & (r > c), 1.0, 0.0).astype(BF16)
    ones = jnp.where(same, 1.0, 0.0).astype(BF16)
    tri_ones = jnp.concatenate([tri, ones], axis=1)

    t_row = lax.broadcasted_iota(jnp.int32, (Q_BLOCK, W), 0)
    s_col = lax.broadcasted_iota(jnp.int32, (Q_BLOCK, W), 1)
    s_col = jnp.where(s_col >= Q_BLOCK, s_col - Q_BLOCK, s_col)
    causal = s_col < t_row

    d_row = lax.broadcasted_iota(jnp.int32, (SB_PAIR, Q_BLOCK), 0)
    d_col = lax.broadcasted_iota(jnp.int32, (Q_BLOCK, SB_PAIR), 1)

    acc[...] = jnp.zeros_like(acc)
    csum[...] = jnp.zeros_like(csum)

    def tile(j, masked):
        k0 = pl.multiple_of(j * Q_BLOCK, Q_BLOCK)
        for hp in range(SB_PAIRS):
            cols = slice(hp * SB_PAIR, (hp + 1) * SB_PAIR)
            q2 = q_ref[:, cols]
            kt = kt_ref[j, cols, :]
            zero = jnp.zeros_like(kt)
            w_qk = jnp.concatenate([jnp.where(d_row < SB_HEAD_DIM, kt, zero),
                                    jnp.where(d_row >= SB_HEAD_DIM, kt, zero)], axis=1)
            z = jnp.dot(q2, w_qk, preferred_element_type=F32)
            logsig = jnp.minimum(z, 0.0) - jnp.log(1.0 + jnp.exp(-jnp.abs(z)))
            log1m = logsig - z
            if masked:
                log1m = jnp.where(causal, log1m, 0.0)
            es = jnp.dot(log1m.astype(BF16), tri_ones, preferred_element_type=F32)
            later = csum[hp]
            w = jnp.exp(logsig + es[:, :W] + later)
            if masked:
                w = jnp.where(causal, w, 0.0)
            v2 = v_ref[pl.ds(k0, Q_BLOCK), cols]
            zv = jnp.zeros_like(v2)
            w_v = jnp.concatenate([jnp.where(d_col < SB_HEAD_DIM, v2, zv),
                                   jnp.where(d_col >= SB_HEAD_DIM, v2, zv)], axis=0)
            acc[hp] += jnp.dot(w.astype(BF16), w_v, preferred_element_type=F32)
            csum[hp] = later + es[:, W:]

    tile(i, True)

    def body(jj, carry):
        tile(i - 1 - jj, False)
        return carry

    lax.fori_loop(0, i, body, 0)

    for hp in range(SB_PAIRS):
        o_ref[:, hp * SB_PAIR:(hp + 1) * SB_PAIR] = acc[hp].astype(BF16)


def _sb_call(q, kt, v):
    B, L, D = q.shape
    nq = L // Q_BLOCK
    return pl.pallas_call(
        _sb_kernel,
        grid=(B, nq),
        in_specs=[pl.BlockSpec((None, Q_BLOCK, D), lambda b, i: (b, i, 0)),
                  pl.BlockSpec((None, nq, D, Q_BLOCK), lambda b, i: (b, 0, 0, 0)),
                  pl.BlockSpec((None, L, D), lambda b, i: (b, 0, 0))],
        out_specs=pl.BlockSpec((None, Q_BLOCK, D), lambda b, i: (b, i, 0)),
        out_shape=jax.ShapeDtypeStruct((B, L, D), BF16),
        scratch_shapes=[pltpu.VMEM((SB_PAIRS, Q_BLOCK, SB_PAIR), F32),
                        pltpu.VMEM((SB_PAIRS, Q_BLOCK, 2 * Q_BLOCK), F32)],
        compiler_params=_params(2),
        name="sb_attn",
    )(q, kt, v)


def _sg_kernel(x_ref, g_ref, win_ref, vg_ref, ws_ref, bst_ref, m_ref, u_s, v_s, *, tl):
    xn = _rms(x_ref[...], g_ref[...]).astype(BF16)
    h = _gelu(jnp.dot(xn, win_ref[...], preferred_element_type=F32))
    u_s[...] = h[:, :D_MODEL]
    v_s[...] = _rms(h[:, D_MODEL:], vg_ref[...]).astype(BF16)
    r = lax.broadcasted_iota(jnp.int32, (CHUNK, CHUNK), 0)
    c = lax.broadcasted_iota(jnp.int32, (CHUNK, CHUNK), 1)
    for g in range(SG_GROUPS):
        cols = slice(g * SG_HEAD_DIM, (g + 1) * SG_HEAD_DIM)
        w = jnp.where(r >= c, ws_ref[g], 0.0).astype(BF16)
        bias = jnp.broadcast_to(bst_ref[:, g:g + 1], (CHUNK, SG_HEAD_DIM))
        for ch in range(tl // CHUNK):
            rows = slice(ch * CHUNK, (ch + 1) * CHUNK)
            sv = jnp.dot(w, v_s[rows, cols], preferred_element_type=F32) + bias
            m_ref[rows, cols] = (u_s[rows, cols] * sv).astype(BF16)


def _sg_call(x, g, win, vg, ws, bst, *, tl=SG_ROWS):
    B, L, D = x.shape
    row = lambda b, i: (b, i, 0)
    return pl.pallas_call(
        functools.partial(_sg_kernel, tl=tl),
        grid=(B, L // tl),
        in_specs=[pl.BlockSpec((None, tl, D), row), _resident(g.shape), _resident(win.shape),
                  _resident(vg.shape), _resident(ws.shape), _resident(bst.shape)],
        out_specs=pl.BlockSpec((None, tl, D), row),
        out_shape=jax.ShapeDtypeStruct((B, L, D), BF16),
        scratch_shapes=[pltpu.VMEM((tl, D), F32), pltpu.VMEM((tl, D), BF16)],
        compiler_params=_params(2),
        name="sg_gate",
    )(x, g, win, vg, ws, bst)


def _ssm_prep_kernel(lre_ref, lim_ref, ldt_ref, cre_ref, cim_ref, ar_ref, ai_ref, ctr_ref, cti_ref):
    lr = jnp.minimum(lre_ref[...], -1e-4)
    li = lim_ref[...]
    dt = jnp.exp(ldt_ref[...])
    mag = jnp.exp(dt * lr)
    ar = mag * jnp.cos(dt * li)
    ai = mag * jnp.sin(dt * li)
    den = lr * lr + li * li
    cr = ((ar - 1.0) * lr + ai * li) / den
    ci = (ai * lr - (ar - 1.0) * li) / den
    ar_ref[...] = ar
    ai_ref[...] = ai
    cre = cre_ref[...]
    cim = cim_ref[...]
    cr3 = cr[:, None, :]
    ci3 = ci[:, None, :]
    ctr_ref[...] = cre * cr3 - cim * ci3
    cti_ref[...] = cre * ci3 + cim * cr3


def _ssm_prep_call(lam_re, lam_im, log_dt, c_re, c_im):
    G, P = lam_re.shape
    H = c_re.shape[1]
    return pl.pallas_call(
        _ssm_prep_kernel,
        out_shape=[jax.ShapeDtypeStruct((G, P), F32), jax.ShapeDtypeStruct((G, P), F32),
                   jax.ShapeDtypeStruct((G, H, P), F32), jax.ShapeDtypeStruct((G, H, P), F32)],
        name="ssm_prep",
    )(lam_re, lam_im, log_dt.reshape(G, 1), c_re, c_im)


def _ssm_kernel(x_ref, g_ref, win_ref, wb_ref, wc_ref, ar_ref, ai_ref, d_ref, o_ref,
                u_lb, bu, y_lb, st, *, tl, nb):
    S = SSM_SLAB_STATES

    @pl.when(pl.program_id(0) == 0)
    def _():
        st[...] = jnp.zeros_like(st)

    for b in range(nb):
        xn = _rms(x_ref[b], g_ref[...]).astype(BF16)
        u = jnp.dot(xn, win_ref[...], preferred_element_type=F32)
        for j in range(SSM_SLABS):
            u_lb[j, pl.ds(b, tl, stride=nb), :] = u[:, j * LANES:(j + 1) * LANES]

    for j in range(SSM_SLABS):
        uj = u_lb[j]
        bu[...] = jnp.dot(uj.astype(BF16), wb_ref[j], preferred_element_type=F32)
        ar = jnp.broadcast_to(ar_ref[j:j + 1, :], (nb, S))
        ai = jnp.broadcast_to(ai_ref[j:j + 1, :], (nb, S))

        def step(t, carry):
            s_re, s_im = carry
            r0 = pl.multiple_of(t * nb, nb)
            n_re = ar * s_re - ai * s_im + bu[pl.ds(r0, nb), 0:S]
            n_im = ar * s_im + ai * s_re + bu[pl.ds(r0, nb), S:2 * S]
            bu[pl.ds(r0, nb), 0:S] = n_re
            bu[pl.ds(r0, nb), S:2 * S] = n_im
            return n_re, n_im

        s_re, s_im = lax.fori_loop(0, tl, step, (st[j, :, 0:S], st[j, :, S:2 * S]), unroll=8)
        st[j, :, 0:S] = s_re
        st[j, :, S:2 * S] = s_im
        y = jnp.dot(bu[...].astype(BF16), wc_ref[j], preferred_element_type=F32)
        y_lb[j] = _gelu(y + d_ref[j:j + 1, :] * uj)

    for b in range(nb):
        for j in range(SSM_SLABS):
            o_ref[b, :, j * LANES:(j + 1) * LANES] = y_lb[j, pl.ds(b, tl, stride=nb), :].astype(BF16)


def _ssm_call(x, g, win, wb, wc, ar, ai, d, *, tl=SSM_STEPS):
    B, L, D = x.shape
    R = B * tl
    blk = lambda i: (0, i, 0)
    return pl.pallas_call(
        functools.partial(_ssm_kernel, tl=tl, nb=B),
        grid=(L // tl,),
        in_specs=[pl.BlockSpec((B, tl, D), blk), _resident(g.shape), _resident(win.shape),
                  _resident(wb.shape), _resident(wc.shape), _resident(ar.shape), _resident(ai.shape),
                  _resident(d.shape)],
        out_specs=pl.BlockSpec((B, tl, D), blk),
        out_shape=jax.ShapeDtypeStruct((B, L, D), BF16),
        scratch_shapes=[pltpu.VMEM((SSM_SLABS, R, LANES), F32),
                        pltpu.VMEM((R, 2 * SSM_SLAB_STATES), F32),
                        pltpu.VMEM((SSM_SLABS, R, LANES), F32),
                        pltpu.VMEM((SSM_SLABS, B, 2 * SSM_SLAB_STATES), F32)],
        compiler_params=_params(1),
        name="ssm_scan",
    )(x, g, win, wb, wc, ar, ai, d)


def _ssm_block_weights(b_re, b_im, ct_re, ct_im):
    J, GL, P, H = SSM_SLABS, SSM_SLAB_GROUPS, SSM_STATE, SSM_GROUP
    eye = jnp.eye(GL, dtype=F32)

    def in_side(b):
        bt = b.reshape(J, GL, P, H).transpose(0, 1, 3, 2)
        return (bt[:, :, :, None, :] * eye[None, :, None, :, None]).reshape(J, GL * H, GL * P)

    def out_side(c):
        ct = c.reshape(J, GL, H, P).transpose(0, 1, 3, 2)
        return (ct[:, :, :, None, :] * eye[None, :, None, :, None]).reshape(J, GL * P, GL * H)

    wb = jnp.concatenate([in_side(b_re), in_side(b_im)], axis=2).astype(BF16)
    wc = jnp.concatenate([out_side(ct_re), -out_side(ct_im)], axis=1).astype(BF16)
    return wb, wc


def kernel(x, norm_g, final_norm_g, sb_w_qkv, sb_w_o, sg_w_in, sg_norm_g, sg_w_s, sg_b, sg_w_o,
           ssm_w_in, ssm_lam_re, ssm_lam_im, ssm_log_dt, ssm_b_re, ssm_b_im, ssm_c_re, ssm_c_im,
           ssm_d, ssm_w_glu, ffn_w_up, ffn_conv_w, ffn_conv_b, ffn_w_down):
    D = D_MODEL
    F = D_FF
    for i in range(DEPTH):
        mixer = i % N_MIXERS
        j = i // N_MIXERS
        g1 = norm_g[i, 0].reshape(1, D)
        g2 = norm_g[i, 1].reshape(1, D)
        if mixer == 0:
            w = sb_w_qkv[j]
            q, kt, v = _qkv_call(x, g1, w[:, :D].astype(BF16), w[:, D:2 * D].T.astype(BF16),
                                 w[:, 2 * D:].astype(BF16))
            m = _sb_call(q, kt, v)
            wm = sb_w_o[j].astype(BF16)
            glu = False
        elif mixer == 1:
            m = _sg_call(x, g1, sg_w_in[j].astype(BF16), sg_norm_g[j].reshape(1, D), sg_w_s[j],
                         sg_b[j].T)
            wm = sg_w_o[j].astype(BF16)
            glu = False
        else:
            ar, ai, ct_re, ct_im = _ssm_prep_call(ssm_lam_re[j], ssm_lam_im[j], ssm_log_dt[j],
                                                  ssm_c_re[j], ssm_c_im[j])
            wb, wc = _ssm_block_weights(ssm_b_re[j], ssm_b_im[j], ct_re, ct_im)
            m = _ssm_call(x, g1, ssm_w_in[j].astype(BF16), wb, wc,
                          ar.reshape(SSM_SLABS, SSM_SLAB_STATES), ai.reshape(SSM_SLABS, SSM_SLAB_STATES),
                          ssm_d[j].reshape(SSM_SLABS, LANES))
            wm = ssm_w_glu[j].astype(BF16)
            glu = True
        wu = ffn_w_up[i]
        cw = ffn_conv_w[i]
        cb = ffn_conv_b[i].reshape(1, 2 * F)
        fg = final_norm_g.reshape(1, D) if i == DEPTH - 1 else None
        x = _ffn_call(x, m, wm, g2, wu[:, :F].astype(BF16), wu[:, F:].astype(BF16),
                      cw[:, :F], cw[:, F:], cb[:, :F], cb[:, F:], ffn_w_down[i].astype(BF16), fg,
                      glu=glu)
    return x
```

```python
import functools
import math

import jax
import jax.numpy as jnp
from jax import lax
from jax.experimental import pallas as pl
from jax.experimental.pallas import tpu as pltpu

F32 = jnp.float32
BF16 = jnp.bfloat16

D_MODEL = 1024
EPS = 1e-6
LOG2E = math.log2(math.e)
SB_HEAD_DIM = 64
SB_PAIR = 2 * SB_HEAD_DIM
SB_PAIRS = D_MODEL // SB_PAIR
Q_BLOCK = 128
MASKED_LOGIT = -1e30
CHUNK = 128
SG_GROUPS = 8
SG_HEAD_DIM = D_MODEL // SG_GROUPS
SSM_GROUP = 16
SSM_GROUPS = 64
SSM_STATE = 64
SSM_SLABS = 8
SSM_SLAB_GROUPS = SSM_GROUPS // SSM_SLABS
SSM_SLAB_STATES = SSM_SLAB_GROUPS * SSM_STATE
D_FF = 2816
CONV_K = 3
DEPTH = 4
N_MIXERS = 3

V7X_VMEM_LIMIT_BYTES = 56 * 1024 * 1024
SUBLANES = 8
LANES = 128

FFN_ROWS = 512
FFN_FC = 256
QKV_ROWS = 512
SG_ROWS = 512
SSM_STEPS = 64


def _rms(x, g):
    return x * lax.rsqrt(jnp.mean(x * x, axis=-1, keepdims=True) + EPS) * g


def _sigmoid(x):
    return 1.0 / (1.0 + jnp.exp(-x))


def _gelu(x):
    c = math.sqrt(2.0 / math.pi)
    return 0.5 * x * (1.0 + jnp.tanh(c * (x + 0.044715 * (x * x * x))))


def _resident(shape):
    nd = len(shape)
    return pl.BlockSpec(shape, lambda *_: (0,) * nd, pipeline_mode=pl.Buffered(1))


def _params(n_axes):
    return pltpu.CompilerParams(
        dimension_semantics=("arbitrary",) * n_axes,
        vmem_limit_bytes=V7X_VMEM_LIMIT_BYTES,
    )


def _ffn_kernel(*refs, glu, final, tl, fc, n_chunks):
    if final:
        (x_ref, m_ref, wm_ref, g_ref, wa_ref, wg_ref, cwa_ref, cwg_ref, cba_ref, cbg_ref, wd_ref,
         fg_ref, o_ref, xn_s, hsa, hsg, cara, carg, act) = refs
    else:
        (x_ref, m_ref, wm_ref, g_ref, wa_ref, wg_ref, cwa_ref, cwg_ref, cba_ref, cbg_ref, wd_ref,
         o_ref, xn_s, hsa, hsg, cara, carg, act) = refs
        fg_ref = None

    @pl.when(pl.program_id(1) == 0)
    def _():
        cara[...] = jnp.zeros_like(cara)
        carg[...] = jnp.zeros_like(carg)

    mm = jnp.dot(m_ref[...], wm_ref[...], preferred_element_type=F32)
    if glu:
        mix = mm[:, :D_MODEL] * _sigmoid(mm[:, D_MODEL:])
    else:
        mix = mm
    x1 = x_ref[...] + mix
    o_ref[...] = x1
    xn_s[...] = _rms(x1, g_ref[...]).astype(BF16)

    def conv(hs, car, cw_ref, cb_ref, h, cols):
        hs[0:SUBLANES, :] = car[:, cols]
        hs[SUBLANES:SUBLANES + tl, :] = h
        car[:, cols] = h[tl - SUBLANES:tl, :]
        w = cw_ref[:, cols]
        y = (w[2:3, :] * hs[SUBLANES:SUBLANES + tl, :]
             + w[1:2, :] * hs[SUBLANES - 1:SUBLANES - 1 + tl, :]
             + w[0:1, :] * hs[SUBLANES - 2:SUBLANES - 2 + tl, :])
        return y + cb_ref[:, cols]

    for c in range(n_chunks):
        cols = slice(c * fc, (c + 1) * fc)
        xn = xn_s[...]
        ha = jnp.dot(xn, wa_ref[:, cols], preferred_element_type=F32)
        hg = jnp.dot(xn, wg_ref[:, cols], preferred_element_type=F32)
        ya = conv(hsa, cara, cwa_ref, cba_ref, ha, cols)
        yg = conv(hsg, carg, cwg_ref, cbg_ref, hg, cols)
        act[:, cols] = (yg * _sigmoid(yg) * ya).astype(BF16)

    out = o_ref[...] + jnp.dot(act[...], wd_ref[...], preferred_element_type=F32)
    if final:
        out = _rms(out, fg_ref[...])
    o_ref[...] = out


def _ffn_call(x, m, wm, g, wa, wg, cwa, cwg, cba, cbg, wd, fg, *, glu, tl=FFN_ROWS, fc=FFN_FC):
    B, L, D = x.shape
    F = wa.shape[1]
    dm = m.shape[2]
    final = fg is not None
    n_chunks = F // fc
    row = lambda b, i: (b, i, 0)
    in_specs = [
        pl.BlockSpec((None, tl, D), row),
        pl.BlockSpec((None, tl, dm), row),
        _resident(wm.shape), _resident(g.shape), _resident(wa.shape), _resident(wg.shape),
        _resident(cwa.shape), _resident(cwg.shape), _resident(cba.shape), _resident(cbg.shape),
        _resident(wd.shape),
    ]
    args = [x, m, wm, g, wa, wg, cwa, cwg, cba, cbg, wd]
    if final:
        in_specs.append(_resident(fg.shape))
        args.append(fg)
    return pl.pallas_call(
        functools.partial(_ffn_kernel, glu=glu, final=final, tl=tl, fc=fc, n_chunks=n_chunks),
        grid=(B, L // tl),
        in_specs=in_specs,
        out_specs=pl.BlockSpec((None, tl, D), row),
        out_shape=jax.ShapeDtypeStruct((B, L, D), F32),
        scratch_shapes=[
            pltpu.VMEM((tl, D), BF16),
            pltpu.VMEM((tl + SUBLANES, fc), F32),
            pltpu.VMEM((tl + SUBLANES, fc), F32),
            pltpu.VMEM((SUBLANES, F), F32),
            pltpu.VMEM((SUBLANES, F), F32),
            pltpu.VMEM((tl, F), BF16),
        ],
        compiler_params=_params(2),
        name="ffn",
    )(*args)


def _qkv_kernel(x_ref, g_ref, wq_ref, wkt_ref, wv_ref, q_ref, kt_ref, v_ref, *, tl):
    xn = _rms(x_ref[...], g_ref[...]).astype(BF16)
    q = jnp.dot(xn, wq_ref[...], preferred_element_type=F32) * (SB_HEAD_DIM ** -0.5)
    q_ref[...] = q.astype(BF16)
    v_ref[...] = jnp.dot(xn, wv_ref[...], preferred_element_type=F32).astype(BF16)
    kt = lax.dot_general(wkt_ref[...], xn, (((1,), (1,)), ((), ())), preferred_element_type=F32)
    kt = kt.astype(BF16)
    for t in range(tl // Q_BLOCK):
        kt_ref[t] = kt[:, t * Q_BLOCK:(t + 1) * Q_BLOCK]


def _qkv_call(x, g, wq, wkt, wv, *, tl=QKV_ROWS):
    B, L, D = x.shape
    nk = tl // Q_BLOCK
    row = lambda b, i: (b, i, 0)
    return pl.pallas_call(
        functools.partial(_qkv_kernel, tl=tl),
        grid=(B, L // tl),
        in_specs=[pl.BlockSpec((None, tl, D), row), _resident(g.shape), _resident(wq.shape),
                  _resident(wkt.shape), _resident(wv.shape)],
        out_specs=[pl.BlockSpec((None, tl, D), row),
                   pl.BlockSpec((None, nk, D, Q_BLOCK), lambda b, i: (b, i, 0, 0)),
                   pl.BlockSpec((None, tl, D), row)],
        out_shape=[jax.ShapeDtypeStruct((B, L, D), BF16),
                   jax.ShapeDtypeStruct((B, L // Q_BLOCK, D, Q_BLOCK), BF16),
                   jax.ShapeDtypeStruct((B, L, D), BF16)],
        compiler_params=_params(2),
        name="sb_qkv",
    )(x, g, wq, wkt, wv)


def _sb_kernel(q_ref, kt_ref, v_ref, o_ref, acc, csum, kbd, vbd, ls_s, l1_s, *, nk):
    i = pl.program_id(1)
    W = 2 * Q_BLOCK
    pairs = range(SB_PAIRS)
    col = lambda hp: slice(hp * SB_PAIR, (hp + 1) * SB_PAIR)

    @pl.when(i == 0)
    def _():
        d_row = lax.broadcasted_iota(jnp.int32, (SB_PAIR, Q_BLOCK), 0)
        d_col = lax.broadcasted_iota(jnp.int32, (Q_BLOCK, SB_PAIR), 1)

        def build(jb, carry):
            k0 = pl.multiple_of(jb * Q_BLOCK, Q_BLOCK)
            for hp in pairs:
                kt = kt_ref[jb, col(hp), :]
                zk = jnp.zeros_like(kt)
                kbd[jb, hp] = jnp.concatenate([jnp.where(d_row < SB_HEAD_DIM, kt, zk),
                                               jnp.where(d_row >= SB_HEAD_DIM, kt, zk)], axis=1)
                v2 = v_ref[pl.ds(k0, Q_BLOCK), col(hp)]
                zv = jnp.zeros_like(v2)
                vbd[jb, hp] = jnp.concatenate([jnp.where(d_col < SB_HEAD_DIM, v2, zv),
                                               jnp.where(d_col >= SB_HEAD_DIM, v2, zv)], axis=0)
            return carry

        lax.fori_loop(0, nk, build, 0)

    r = lax.broadcasted_iota(jnp.int32, (W, W), 0)
    c = lax.broadcasted_iota(jnp.int32, (W, W), 1)
    same = (r < Q_BLOCK) == (c < Q_BLOCK)
    tri = jnp.where(same & (r > c), 1.0, 0.0).astype(BF16)
    ones = jnp.where(same, 1.0, 0.0).astype(BF16)
    tri_ones = jnp.concatenate([tri, ones], axis=1)

    t_row = lax.broadcasted_iota(jnp.int32, (Q_BLOCK, W), 0)
    s_col = lax.broadcasted_iota(jnp.int32, (Q_BLOCK, W), 1)
    s_col = jnp.where(s_col >= Q_BLOCK, s_col - Q_BLOCK, s_col)
    causal = s_col < t_row

    acc[...] = jnp.zeros_like(acc)
    csum[...] = jnp.zeros_like(csum)

    def scores(j, hp, masked):
        z = jnp.dot(q_ref[:, col(hp)], kbd[j, hp], preferred_element_type=F32)
        ls = jnp.minimum(z, 0.0) - jnp.log(1.0 + jnp.exp2(jnp.abs(z) * -LOG2E))
        l1 = ls - z
        if masked:
            l1 = jnp.where(causal, l1, 0.0)
            ls = jnp.where(causal, ls, MASKED_LOGIT)
        return ls, l1.astype(BF16)

    def weights(hp):
        es = jnp.dot(l1_s[hp], tri_ones, preferred_element_type=F32)
        later = csum[hp]
        w = jnp.exp(ls_s[hp] + es[:, :W] + later)
        csum[hp] = later + es[:, W:]
        return w.astype(BF16)

    def values(j, wts):
        for hp in pairs:
            acc[hp] += jnp.dot(wts[hp], vbd[j, hp], preferred_element_type=F32)

    for hp in pairs:
        ls_s[hp], l1_s[hp] = scores(i, hp, True)

    def body(jj, carry):
        j = i - jj
        wts = []
        for hp in pairs:
            wts.append(weights(hp))
            ls_s[hp], l1_s[hp] = scores(j - 1, hp, False)
        values(j, wts)
        return carry

    lax.fori_loop(0, i, body, 0)
    values(0, [weights(hp) for hp in pairs])

    for hp in pairs:
        o_ref[:, col(hp)] = acc[hp].astype(BF16)


def _sb_call(q, kt, v):
    B, L, D = q.shape
    nq = L // Q_BLOCK
    return pl.pallas_call(
        functools.partial(_sb_kernel, nk=nq),
        grid=(B, nq),
        in_specs=[pl.BlockSpec((None, Q_BLOCK, D), lambda b, i: (b, i, 0)),
                  pl.BlockSpec((None, nq, D, Q_BLOCK), lambda b, i: (b, 0, 0, 0)),
                  pl.BlockSpec((None, L, D), lambda b, i: (b, 0, 0))],
        out_specs=pl.BlockSpec((None, Q_BLOCK, D), lambda b, i: (b, i, 0)),
        out_shape=jax.ShapeDtypeStruct((B, L, D), BF16),
        scratch_shapes=[pltpu.VMEM((SB_PAIRS, Q_BLOCK, SB_PAIR), F32),
                        pltpu.VMEM((SB_PAIRS, Q_BLOCK, 2 * Q_BLOCK), F32),
                        pltpu.VMEM((nq, SB_PAIRS, SB_PAIR, 2 * Q_BLOCK), BF16),
                        pltpu.VMEM((nq, SB_PAIRS, 2 * Q_BLOCK, SB_PAIR), BF16),
                        pltpu.VMEM((SB_PAIRS, Q_BLOCK, 2 * Q_BLOCK), F32),
                        pltpu.VMEM((SB_PAIRS, Q_BLOCK, 2 * Q_BLOCK), BF16)],
        compiler_params=_params(2),
        name="sb_attn",
    )(q, kt, v)


def _sg_kernel(x_ref, g_ref, win_ref, vg_ref, ws_ref, bst_ref, m_ref, u_s, v_s, *, tl):
    xn = _rms(x_ref[...], g_ref[...]).astype(BF16)
    h = _gelu(jnp.dot(xn, win_ref[...], preferred_element_type=F32))
    u_s[...] = h[:, :D_MODEL]
    v_s[...] = _rms(h[:, D_MODEL:], vg_ref[...]).astype(BF16)
    r = lax.broadcasted_iota(jnp.int32, (CHUNK, CHUNK), 0)
    c = lax.broadcasted_iota(jnp.int32, (CHUNK, CHUNK), 1)
    for g in range(SG_GROUPS):
        cols = slice(g * SG_HEAD_DIM, (g + 1) * SG_HEAD_DIM)
        w = jnp.where(r >= c, ws_ref[g], 0.0).astype(BF16)
        bias = jnp.broadcast_to(bst_ref[:, g:g + 1], (CHUNK, SG_HEAD_DIM))
        for ch in range(tl // CHUNK):
            rows = slice(ch * CHUNK, (ch + 1) * CHUNK)
            sv = jnp.dot(w, v_s[rows, cols], preferred_element_type=F32) + bias
            m_ref[rows, cols] = (u_s[rows, cols] * sv).astype(BF16)


def _sg_call(x, g, win, vg, ws, bst, *, tl=SG_ROWS):
    B, L, D = x.shape
    row = lambda b, i: (b, i, 0)
    return pl.pallas_call(
        functools.partial(_sg_kernel, tl=tl),
        grid=(B, L // tl),
        in_specs=[pl.BlockSpec((None, tl, D), row), _resident(g.shape), _resident(win.shape),
                  _resident(vg.shape), _resident(ws.shape), _resident(bst.shape)],
        out_specs=pl.BlockSpec((None, tl, D), row),
        out_shape=jax.ShapeDtypeStruct((B, L, D), BF16),
        scratch_shapes=[pltpu.VMEM((tl, D), F32), pltpu.VMEM((tl, D), BF16)],
        compiler_params=_params(2),
        name="sg_gate",
    )(x, g, win, vg, ws, bst)


def _ssm_prep_kernel(lre_ref, lim_ref, ldt_ref, cre_ref, cim_ref, ar_ref, ai_ref, ctr_ref, cti_ref):
    lr = jnp.minimum(lre_ref[...], -1e-4)
    li = lim_ref[...]
    dt = jnp.exp(ldt_ref[...])
    mag = jnp.exp(dt * lr)
    ar = mag * jnp.cos(dt * li)
    ai = mag * jnp.sin(dt * li)
    den = lr * lr + li * li
    cr = ((ar - 1.0) * lr + ai * li) / den
    ci = (ai * lr - (ar - 1.0) * li) / den
    ar_ref[...] = ar
    ai_ref[...] = ai
    cre = cre_ref[...]
    cim = cim_ref[...]
    cr3 = cr[:, None, :]
    ci3 = ci[:, None, :]
    ctr_ref[...] = cre * cr3 - cim * ci3
    cti_ref[...] = cre * ci3 + cim * cr3


def _ssm_prep_call(lam_re, lam_im, log_dt, c_re, c_im):
    G, P = lam_re.shape
    H = c_re.shape[1]
    return pl.pallas_call(
        _ssm_prep_kernel,
        out_shape=[jax.ShapeDtypeStruct((G, P), F32), jax.ShapeDtypeStruct((G, P), F32),
                   jax.ShapeDtypeStruct((G, H, P), F32), jax.ShapeDtypeStruct((G, H, P), F32)],
        name="ssm_prep",
    )(lam_re, lam_im, log_dt.reshape(G, 1), c_re, c_im)


def _ssm_kernel(x_ref, g_ref, win_ref, wb_ref, wc_ref, ar_ref, ai_ref, d_ref, o_ref,
                u_lb, bu, y_lb, st, *, tl, nb):
    S = SSM_SLAB_STATES

    @pl.when(pl.program_id(0) == 0)
    def _():
        st[...] = jnp.zeros_like(st)

    for b in range(nb):
        xn = _rms(x_ref[b], g_ref[...]).astype(BF16)
        u = jnp.dot(xn, win_ref[...], preferred_element_type=F32)
        for j in range(SSM_SLABS):
            u_lb[j, pl.ds(b, tl, stride=nb), :] = u[:, j * LANES:(j + 1) * LANES]

    for j in range(SSM_SLABS):
        uj = u_lb[j]
        bu[...] = jnp.dot(uj.astype(BF16), wb_ref[j], preferred_element_type=F32)
        ar = jnp.broadcast_to(ar_ref[j:j + 1, :], (nb, S))
        ai = jnp.broadcast_to(ai_ref[j:j + 1, :], (nb, S))

        def step(t, carry):
            s_re, s_im = carry
            r0 = pl.multiple_of(t * nb, nb)
            n_re = ar * s_re - ai * s_im + bu[pl.ds(r0, nb), 0:S]
            n_im = ar * s_im + ai * s_re + bu[pl.ds(r0, nb), S:2 * S]
            bu[pl.ds(r0, nb), 0:S] = n_re
            bu[pl.ds(r0, nb), S:2 * S] = n_im
            return n_re, n_im

        s_re, s_im = lax.fori_loop(0, tl, step, (st[j, :, 0:S], st[j, :, S:2 * S]), unroll=True)
        st[j, :, 0:S] = s_re
        st[j, :, S:2 * S] = s_im
        y = jnp.dot(bu[...].astype(BF16), wc_ref[j], preferred_element_type=F32)
        y_lb[j] = _gelu(y + d_ref[j:j + 1, :] * uj)

    for b in range(nb):
        for j in range(SSM_SLABS):
            o_ref[b, :, j * LANES:(j + 1) * LANES] = y_lb[j, pl.ds(b, tl, stride=nb), :].astype(BF16)


def _ssm_call(x, g, win, wb, wc, ar, ai, d, *, tl=SSM_STEPS):
    B, L, D = x.shape
    R = B * tl
    blk = lambda i: (0, i, 0)
    return pl.pallas_call(
        functools.partial(_ssm_kernel, tl=tl, nb=B),
        grid=(L // tl,),
        in_specs=[pl.BlockSpec((B, tl, D), blk), _resident(g.shape), _resident(win.shape),
                  _resident(wb.shape), _resident(wc.shape), _resident(ar.shape), _resident(ai.shape),
                  _resident(d.shape)],
        out_specs=pl.BlockSpec((B, tl, D), blk),
        out_shape=jax.ShapeDtypeStruct((B, L, D), BF16),
        scratch_shapes=[pltpu.VMEM((SSM_SLABS, R, LANES), F32),
                        pltpu.VMEM((R, 2 * SSM_SLAB_STATES), F32),
                        pltpu.VMEM((SSM_SLABS, R, LANES), F32),
                        pltpu.VMEM((SSM_SLABS, B, 2 * SSM_SLAB_STATES), F32)],
        compiler_params=_params(1),
        name="ssm_scan",
    )(x, g, win, wb, wc, ar, ai, d)


def _ssm_block_weights(b_re, b_im, ct_re, ct_im):
    J, GL, P, H = SSM_SLABS, SSM_SLAB_GROUPS, SSM_STATE, SSM_GROUP
    eye = jnp.eye(GL, dtype=F32)

    def in_side(b):
        bt = b.reshape(J, GL, P, H).transpose(0, 1, 3, 2)
        return (bt[:, :, :, None, :] * eye[None, :, None, :, None]).reshape(J, GL * H, GL * P)

    def out_side(c):
        ct = c.reshape(J, GL, H, P).transpose(0, 1, 3, 2)
        return (ct[:, :, :, None, :] * eye[None, :, None, :, None]).reshape(J, GL * P, GL * H)

    wb = jnp.concatenate([in_side(b_re), in_side(b_im)], axis=2).astype(BF16)
    wc = jnp.concatenate([out_side(ct_re), -out_side(ct_im)], axis=1).astype(BF16)
    return wb, wc


def kernel(x, norm_g, final_norm_g, sb_w_qkv, sb_w_o, sg_w_in, sg_norm_g, sg_w_s, sg_b, sg_w_o,
           ssm_w_in, ssm_lam_re, ssm_lam_im, ssm_log_dt, ssm_b_re, ssm_b_im, ssm_c_re, ssm_c_im,
           ssm_d, ssm_w_glu, ffn_w_up, ffn_conv_w, ffn_conv_b, ffn_w_down):
    D = D_MODEL
    F = D_FF
    for i in range(DEPTH):
        mixer = i % N_MIXERS
        j = i // N_MIXERS
        g1 = norm_g[i, 0].reshape(1, D)
        g2 = norm_g[i, 1].reshape(1, D)
        if mixer == 0:
            w = sb_w_qkv[j]
            q, kt, v = _qkv_call(x, g1, w[:, :D].astype(BF16), w[:, D:2 * D].T.astype(BF16),
                                 w[:, 2 * D:].astype(BF16))
            m = _sb_call(q, kt, v)
            wm = sb_w_o[j].astype(BF16)
            glu = False
        elif mixer == 1:
            m = _sg_call(x, g1, sg_w_in[j].astype(BF16), sg_norm_g[j].reshape(1, D), sg_w_s[j],
                         sg_b[j].T)
            wm = sg_w_o[j].astype(BF16)
            glu = False
        else:
            ar, ai, ct_re, ct_im = _ssm_prep_call(ssm_lam_re[j], ssm_lam_im[j], ssm_log_dt[j],
                                                  ssm_c_re[j], ssm_c_im[j])
            wb, wc = _ssm_block_weights(ssm_b_re[j], ssm_b_im[j], ct_re, ct_im)
            m = _ssm_call(x, g1, ssm_w_in[j].astype(BF16), wb, wc,
                          ar.reshape(SSM_SLABS, SSM_SLAB_STATES), ai.reshape(SSM_SLABS, SSM_SLAB_STATES),
                          ssm_d[j].reshape(SSM_SLABS, LANES))
            wm = ssm_w_glu[j].astype(BF16)
            glu = True
        wu = ffn_w_up[i]
        cw = ffn_conv_w[i]
        cb = ffn_conv_b[i].reshape(1, 2 * F)
        fg = final_norm_g.reshape(1, D) if i == DEPTH - 1 else None
        x = _ffn_call(x, m, wm, g2, wu[:, :F].astype(BF16), wu[:, F:].astype(BF16),
                      cw[:, :F], cw[:, F:], cb[:, :F], cb[:, F:], ffn_w_down[i].astype(BF16), fg,
                      glu=glu)
    return x
```

```python
import functools
import math

import jax
import jax.numpy as jnp
from jax import lax
from jax.experimental import pallas as pl
from jax.experimental.pallas import tpu as pltpu

F32 = jnp.float32
BF16 = jnp.bfloat16

D_MODEL = 1024
EPS = 1e-6
LOG2E = math.log2(math.e)
SB_HEAD_DIM = 64
SB_PAIR = 2 * SB_HEAD_DIM
SB_PAIRS = D_MODEL // SB_PAIR
Q_BLOCK = 128
MASKED_LOGIT = -1e30
EXIT_LOGIT = -110.0
CHUNK = 128
SG_GROUPS = 8
SG_HEAD_DIM = D_MODEL // SG_GROUPS
SSM_GROUP = 16
SSM_GROUPS = 64
SSM_STATE = 64
SSM_SLABS = 8
SSM_SLAB_GROUPS = SSM_GROUPS // SSM_SLABS
SSM_SLAB_STATES = SSM_SLAB_GROUPS * SSM_STATE
D_FF = 2816
CONV_K = 3
DEPTH = 4
N_MIXERS = 3

V7X_VMEM_LIMIT_BYTES = 56 * 1024 * 1024
SUBLANES = 8
LANES = 128

FFN_ROWS = 512
FFN_FC = 256
QKV_ROWS = 512
SG_ROWS = 512
SSM_STEPS = 64


def _rms(x, g):
    return x * lax.rsqrt(jnp.mean(x * x, axis=-1, keepdims=True) + EPS) * g


def _sigmoid(x):
    return 1.0 / (1.0 + jnp.exp(-x))


def _gelu(x):
    c = math.sqrt(2.0 / math.pi)
    return 0.5 * x * (1.0 + jnp.tanh(c * (x + 0.044715 * (x * x * x))))


def _resident(shape):
    nd = len(shape)
    return pl.BlockSpec(shape, lambda *_: (0,) * nd, pipeline_mode=pl.Buffered(1))


def _params(n_axes):
    return pltpu.CompilerParams(
        dimension_semantics=("arbitrary",) * n_axes,
        vmem_limit_bytes=V7X_VMEM_LIMIT_BYTES,
    )


def _ffn_kernel(*refs, glu, final, tl, fc, n_chunks):
    if final:
        (x_ref, m_ref, wm_ref, g_ref, wa_ref, wg_ref, cwa_ref, cwg_ref, cba_ref, cbg_ref, wd_ref,
         fg_ref, o_ref, xn_s, hsa, hsg, cara, carg, act) = refs
    else:
        (x_ref, m_ref, wm_ref, g_ref, wa_ref, wg_ref, cwa_ref, cwg_ref, cba_ref, cbg_ref, wd_ref,
         o_ref, xn_s, hsa, hsg, cara, carg, act) = refs
        fg_ref = None

    @pl.when(pl.program_id(1) == 0)
    def _():
        cara[...] = jnp.zeros_like(cara)
        carg[...] = jnp.zeros_like(carg)

    mm = jnp.dot(m_ref[...], wm_ref[...], preferred_element_type=F32)
    if glu:
        mix = mm[:, :D_MODEL] * _sigmoid(mm[:, D_MODEL:])
    else:
        mix = mm
    x1 = x_ref[...] + mix
    o_ref[...] = x1
    xn_s[...] = _rms(x1, g_ref[...]).astype(BF16)

    def conv(hs, car, cw_ref, cb_ref, h, cols):
        hs[0:SUBLANES, :] = car[:, cols]
        hs[SUBLANES:SUBLANES + tl, :] = h
        car[:, cols] = h[tl - SUBLANES:tl, :]
        w = cw_ref[:, cols]
        y = (w[2:3, :] * hs[SUBLANES:SUBLANES + tl, :]
             + w[1:2, :] * hs[SUBLANES - 1:SUBLANES - 1 + tl, :]
             + w[0:1, :] * hs[SUBLANES - 2:SUBLANES - 2 + tl, :])
        return y + cb_ref[:, cols]

    for c in range(n_chunks):
        cols = slice(c * fc, (c + 1) * fc)
        xn = xn_s[...]
        ha = jnp.dot(xn, wa_ref[:, cols], preferred_element_type=F32)
        hg = jnp.dot(xn, wg_ref[:, cols], preferred_element_type=F32)
        ya = conv(hsa, cara, cwa_ref, cba_ref, ha, cols)
        yg = conv(hsg, carg, cwg_ref, cbg_ref, hg, cols)
        act[:, cols] = (yg * _sigmoid(yg) * ya).astype(BF16)

    head = (n_chunks - 1) * fc
    out = (o_ref[...] + jnp.dot(act[:, :head], wd_ref[:head, :], preferred_element_type=F32)
           + jnp.dot(act[:, head:], wd_ref[head:, :], preferred_element_type=F32))
    if final:
        out = _rms(out, fg_ref[...])
    o_ref[...] = out


def _ffn_call(x, m, wm, g, wa, wg, cwa, cwg, cba, cbg, wd, fg, *, glu, tl=FFN_ROWS, fc=FFN_FC):
    B, L, D = x.shape
    F = wa.shape[1]
    dm = m.shape[2]
    final = fg is not None
    n_chunks = F // fc
    row = lambda b, i: (b, i, 0)
    in_specs = [
        pl.BlockSpec((None, tl, D), row),
        pl.BlockSpec((None, tl, dm), row),
        _resident(wm.shape), _resident(g.shape), _resident(wa.shape), _resident(wg.shape),
        _resident(cwa.shape), _resident(cwg.shape), _resident(cba.shape), _resident(cbg.shape),
        _resident(wd.shape),
    ]
    args = [x, m, wm, g, wa, wg, cwa, cwg, cba, cbg, wd]
    if final:
        in_specs.append(_resident(fg.shape))
        args.append(fg)
    return pl.pallas_call(
        functools.partial(_ffn_kernel, glu=glu, final=final, tl=tl, fc=fc, n_chunks=n_chunks),
        grid=(B, L // tl),
        in_specs=in_specs,
        out_specs=pl.BlockSpec((None, tl, D), row),
        out_shape=jax.ShapeDtypeStruct((B, L, D), F32),
        scratch_shapes=[
            pltpu.VMEM((tl, D), BF16),
            pltpu.VMEM((tl + SUBLANES, fc), F32),
            pltpu.VMEM((tl + SUBLANES, fc), F32),
            pltpu.VMEM((SUBLANES, F), F32),
            pltpu.VMEM((SUBLANES, F), F32),
            pltpu.VMEM((tl, F), BF16),
        ],
        compiler_params=_params(2),
        name="ffn",
    )(*args)


def _qkv_kernel(x_ref, g_ref, wq_ref, wkt_ref, wv_ref, q_ref, kt_ref, v_ref, *, tl):
    xn = _rms(x_ref[...], g_ref[...]).astype(BF16)
    q = jnp.dot(xn, wq_ref[...], preferred_element_type=F32) * (SB_HEAD_DIM ** -0.5)
    q_ref[...] = q.astype(BF16)
    v_ref[...] = jnp.dot(xn, wv_ref[...], preferred_element_type=F32).astype(BF16)
    kt = lax.dot_general(wkt_ref[...], xn, (((1,), (1,)), ((), ())), preferred_element_type=F32)
    kt = kt.astype(BF16)
    for t in range(tl // Q_BLOCK):
        kt_ref[t] = kt[:, t * Q_BLOCK:(t + 1) * Q_BLOCK]


def _qkv_call(x, g, wq, wkt, wv, *, tl=QKV_ROWS):
    B, L, D = x.shape
    nk = tl // Q_BLOCK
    row = lambda b, i: (b, i, 0)
    return pl.pallas_call(
        functools.partial(_qkv_kernel, tl=tl),
        grid=(B, L // tl),
        in_specs=[pl.BlockSpec((None, tl, D), row), _resident(g.shape), _resident(wq.shape),
                  _resident(wkt.shape), _resident(wv.shape)],
        out_specs=[pl.BlockSpec((None, tl, D), row),
                   pl.BlockSpec((None, nk, D, Q_BLOCK), lambda b, i: (b, i, 0, 0)),
                   pl.BlockSpec((None, tl, D), row)],
        out_shape=[jax.ShapeDtypeStruct((B, L, D), BF16),
                   jax.ShapeDtypeStruct((B, L // Q_BLOCK, D, Q_BLOCK), BF16),
                   jax.ShapeDtypeStruct((B, L, D), BF16)],
        compiler_params=_params(2),
        name="sb_qkv",
    )(x, g, wq, wkt, wv)


def _sb_kernel(q_ref, kt_ref, v_ref, o_ref, acc, csum, kbd, vbd, ls_s, l1_s, *, nk):
    i = pl.program_id(1)
    W = 2 * Q_BLOCK
    pairs = range(SB_PAIRS)
    col = lambda hp: slice(hp * SB_PAIR, (hp + 1) * SB_PAIR)

    @pl.when(i == 0)
    def _():
        d_row = lax.broadcasted_iota(jnp.int32, (SB_PAIR, Q_BLOCK), 0)
        d_col = lax.broadcasted_iota(jnp.int32, (Q_BLOCK, SB_PAIR), 1)

        def build(jb, carry):
            k0 = pl.multiple_of(jb * Q_BLOCK, Q_BLOCK)
            for hp in pairs:
                kt = kt_ref[jb, col(hp), :]
                zk = jnp.zeros_like(kt)
                kbd[jb, hp] = jnp.concatenate([jnp.where(d_row < SB_HEAD_DIM, kt, zk),
                                               jnp.where(d_row >= SB_HEAD_DIM, kt, zk)], axis=1)
                v2 = v_ref[pl.ds(k0, Q_BLOCK), col(hp)]
                zv = jnp.zeros_like(v2)
                vbd[jb, hp] = jnp.concatenate([jnp.where(d_col < SB_HEAD_DIM, v2, zv),
                                               jnp.where(d_col >= SB_HEAD_DIM, v2, zv)], axis=0)
            return carry

        lax.fori_loop(0, nk, build, 0)

    r = lax.broadcasted_iota(jnp.int32, (W, W), 0)
    c = lax.broadcasted_iota(jnp.int32, (W, W), 1)
    same = (r < Q_BLOCK) == (c < Q_BLOCK)
    tri = jnp.where(same & (r > c), 1.0, 0.0).astype(BF16)
    ones = jnp.where(same, 1.0, 0.0).astype(BF16)
    tri_ones = jnp.concatenate([tri, ones], axis=1)

    t_row = lax.broadcasted_iota(jnp.int32, (Q_BLOCK, W), 0)
    s_col = lax.broadcasted_iota(jnp.int32, (Q_BLOCK, W), 1)
    s_col = jnp.where(s_col >= Q_BLOCK, s_col - Q_BLOCK, s_col)
    causal = s_col < t_row

    acc[...] = jnp.zeros_like(acc)
    csum[...] = jnp.zeros_like(csum)

    def scores(j, hp, masked):
        z = jnp.dot(q_ref[:, col(hp)], kbd[j, hp], preferred_element_type=F32)
        ls = jnp.minimum(z, 0.0) - jnp.log(1.0 + jnp.exp2(jnp.abs(z) * -LOG2E))
        l1 = ls - z
        if masked:
            l1 = jnp.where(causal, l1, 0.0)
            ls = jnp.where(causal, ls, MASKED_LOGIT)
        return ls, l1.astype(BF16)

    def weights(hp):
        es = jnp.dot(l1_s[hp], tri_ones, preferred_element_type=F32)
        later = csum[hp]
        w = jnp.exp(ls_s[hp] + es[:, :W] + later)
        later = later + es[:, W:]
        csum[hp] = later
        return w.astype(BF16), later

    def values(j, wts):
        for hp in pairs:
            acc[hp] += jnp.dot(wts[hp], vbd[j, hp], preferred_element_type=F32)

    for hp in pairs:
        ls_s[hp], l1_s[hp] = scores(i, hp, True)

    def cond(c):
        j, live = c
        return jnp.logical_and(j >= 1, live > 0)

    def body(c):
        j, _ = c
        wts = []
        top = None
        for hp in pairs:
            w, later = weights(hp)
            top = later if top is None else jnp.maximum(top, later)
            wts.append(w)
            ls_s[hp], l1_s[hp] = scores(j - 1, hp, False)
        values(j, wts)
        return j - 1, (jnp.max(top) >= EXIT_LOGIT).astype(jnp.int32)

    _, live = lax.while_loop(cond, body, (i, jnp.int32(1)))

    @pl.when(live > 0)
    def _():
        values(0, [weights(hp)[0] for hp in pairs])

    for hp in pairs:
        o_ref[:, col(hp)] = acc[hp].astype(BF16)


def _sb_call(q, kt, v):
    B, L, D = q.shape
    nq = L // Q_BLOCK
    return pl.pallas_call(
        functools.partial(_sb_kernel, nk=nq),
        grid=(B, nq),
        in_specs=[pl.BlockSpec((None, Q_BLOCK, D), lambda b, i: (b, i, 0)),
                  pl.BlockSpec((None, nq, D, Q_BLOCK), lambda b, i: (b, 0, 0, 0)),
                  pl.BlockSpec((None, L, D), lambda b, i: (b, 0, 0))],
        out_specs=pl.BlockSpec((None, Q_BLOCK, D), lambda b, i: (b, i, 0)),
        out_shape=jax.ShapeDtypeStruct((B, L, D), BF16),
        scratch_shapes=[pltpu.VMEM((SB_PAIRS, Q_BLOCK, SB_PAIR), F32),
                        pltpu.VMEM((SB_PAIRS, Q_BLOCK, 2 * Q_BLOCK), F32),
                        pltpu.VMEM((nq, SB_PAIRS, SB_PAIR, 2 * Q_BLOCK), BF16),
                        pltpu.VMEM((nq, SB_PAIRS, 2 * Q_BLOCK, SB_PAIR), BF16),
                        pltpu.VMEM((SB_PAIRS, Q_BLOCK, 2 * Q_BLOCK), F32),
                        pltpu.VMEM((SB_PAIRS, Q_BLOCK, 2 * Q_BLOCK), BF16)],
        compiler_params=_params(2),
        name="sb_attn",
    )(q, kt, v)


def _sg_kernel(x_ref, g_ref, win_ref, vg_ref, ws_ref, bst_ref, m_ref, u_s, v_s, *, tl):
    xn = _rms(x_ref[...], g_ref[...]).astype(BF16)
    h = _gelu(jnp.dot(xn, win_ref[...], preferred_element_type=F32))
    u_s[...] = h[:, :D_MODEL]
    v_s[...] = _rms(h[:, D_MODEL:], vg_ref[...]).astype(BF16)
    r = lax.broadcasted_iota(jnp.int32, (CHUNK, CHUNK), 0)
    c = lax.broadcasted_iota(jnp.int32, (CHUNK, CHUNK), 1)
    for g in range(SG_GROUPS):
        cols = slice(g * SG_HEAD_DIM, (g + 1) * SG_HEAD_DIM)
        w = jnp.where(r >= c, ws_ref[g], 0.0).astype(BF16)
        bias = jnp.broadcast_to(bst_ref[:, g:g + 1], (CHUNK, SG_HEAD_DIM))
        for ch in range(tl // CHUNK):
            rows = slice(ch * CHUNK, (ch + 1) * CHUNK)
            sv = jnp.dot(w, v_s[rows, cols], preferred_element_type=F32) + bias
            m_ref[rows, cols] = (u_s[rows, cols] * sv).astype(BF16)


def _sg_call(x, g, win, vg, ws, bst, *, tl=SG_ROWS):
    B, L, D = x.shape
    row = lambda b, i: (b, i, 0)
    return pl.pallas_call(
        functools.partial(_sg_kernel, tl=tl),
        grid=(B, L // tl),
        in_specs=[pl.BlockSpec((None, tl, D), row), _resident(g.shape), _resident(win.shape),
                  _resident(vg.shape), _resident(ws.shape), _resident(bst.shape)],
        out_specs=pl.BlockSpec((None, tl, D), row),
        out_shape=jax.ShapeDtypeStruct((B, L, D), BF16),
        scratch_shapes=[pltpu.VMEM((tl, D), F32), pltpu.VMEM((tl, D), BF16)],
        compiler_params=_params(2),
        name="sg_gate",
    )(x, g, win, vg, ws, bst)


def _ssm_prep_kernel(lre_ref, lim_ref, ldt_ref, cre_ref, cim_ref, ar_ref, ai_ref, ctr_ref, cti_ref):
    lr = jnp.minimum(lre_ref[...], -1e-4)
    li = lim_ref[...]
    dt = jnp.exp(ldt_ref[...])
    mag = jnp.exp(dt * lr)
    ar = mag * jnp.cos(dt * li)
    ai = mag * jnp.sin(dt * li)
    den = lr * lr + li * li
    cr = ((ar - 1.0) * lr + ai * li) / den
    ci = (ai * lr - (ar - 1.0) * li) / den
    ar_ref[...] = ar
    ai_ref[...] = ai
    cre = cre_ref[...]
    cim = cim_ref[...]
    cr3 = cr[:, None, :]
    ci3 = ci[:, None, :]
    ctr_ref[...] = cre * cr3 - cim * ci3
    cti_ref[...] = cre * ci3 + cim * cr3


def _ssm_prep_call(lam_re, lam_im, log_dt, c_re, c_im):
    G, P = lam_re.shape
    H = c_re.shape[1]
    return pl.pallas_call(
        _ssm_prep_kernel,
        out_shape=[jax.ShapeDtypeStruct((G, P), F32), jax.ShapeDtypeStruct((G, P), F32),
                   jax.ShapeDtypeStruct((G, H, P), F32), jax.ShapeDtypeStruct((G, H, P), F32)],
        name="ssm_prep",
    )(lam_re, lam_im, log_dt.reshape(G, 1), c_re, c_im)


def _ssm_kernel(x_ref, g_ref, win_ref, wb_ref, wc_ref, ar_ref, ai_ref, d_ref, o_ref,
                u_lb, bu, y_lb, st, *, tl, nb):
    S = SSM_SLAB_STATES

    @pl.when(pl.program_id(0) == 0)
    def _():
        st[...] = jnp.zeros_like(st)

    for b in range(nb):
        xn = _rms(x_ref[b], g_ref[...]).astype(BF16)
        u = jnp.dot(xn, win_ref[...], preferred_element_type=F32)
        for j in range(SSM_SLABS):
            u_lb[j, pl.ds(b, tl, stride=nb), :] = u[:, j * LANES:(j + 1) * LANES]

    for j in range(SSM_SLABS):
        uj = u_lb[j]
        bu[...] = jnp.dot(uj.astype(BF16), wb_ref[j], preferred_element_type=F32)
        ar = jnp.broadcast_to(ar_ref[j:j + 1, :], (nb, S))
        ai = jnp.broadcast_to(ai_ref[j:j + 1, :], (nb, S))

        def step(t, carry):
            s_re, s_im = carry
            r0 = pl.multiple_of(t * nb, nb)
            n_re = ar * s_re - ai * s_im + bu[pl.ds(r0, nb), 0:S]
            n_im = ar * s_im + ai * s_re + bu[pl.ds(r0, nb), S:2 * S]
            bu[pl.ds(r0, nb), 0:S] = n_re
            bu[pl.ds(r0, nb), S:2 * S] = n_im
            return n_re, n_im

        s_re, s_im = lax.fori_loop(0, tl, step, (st[j, :, 0:S], st[j, :, S:2 * S]), unroll=True)
        st[j, :, 0:S] = s_re
        st[j, :, S:2 * S] = s_im
        y = jnp.dot(bu[...].astype(BF16), wc_ref[j], preferred_element_type=F32)
        y_lb[j] = _gelu(y + d_ref[j:j + 1, :] * uj)

    for b in range(nb):
        for j in range(SSM_SLABS):
            o_ref[b, :, j * LANES:(j + 1) * LANES] = y_lb[j, pl.ds(b, tl, stride=nb), :].astype(BF16)


def _ssm_call(x, g, win, wb, wc, ar, ai, d, *, tl=SSM_STEPS):
    B, L, D = x.shape
    R = B * tl
    blk = lambda i: (0, i, 0)
    return pl.pallas_call(
        functools.partial(_ssm_kernel, tl=tl, nb=B),
        grid=(L // tl,),
        in_specs=[pl.BlockSpec((B, tl, D), blk), _resident(g.shape), _resident(win.shape),
                  _resident(wb.shape), _resident(wc.shape), _resident(ar.shape), _resident(ai.shape),
                  _resident(d.shape)],
        out_specs=pl.BlockSpec((B, tl, D), blk),
        out_shape=jax.ShapeDtypeStruct((B, L, D), BF16),
        scratch_shapes=[pltpu.VMEM((SSM_SLABS, R, LANES), F32),
                        pltpu.VMEM((R, 2 * SSM_SLAB_STATES), F32),
                        pltpu.VMEM((SSM_SLABS, R, LANES), F32),
                        pltpu.VMEM((SSM_SLABS, B, 2 * SSM_SLAB_STATES), F32)],
        compiler_params=_params(1),
        name="ssm_scan",
    )(x, g, win, wb, wc, ar, ai, d)


def _ssm_block_weights(b_re, b_im, ct_re, ct_im):
    J, GL, P, H = SSM_SLABS, SSM_SLAB_GROUPS, SSM_STATE, SSM_GROUP
    eye = jnp.eye(GL, dtype=F32)

    def in_side(b):
        bt = b.reshape(J, GL, P, H).transpose(0, 1, 3, 2)
        return (bt[:, :, :, None, :] * eye[None, :, None, :, None]).reshape(J, GL * H, GL * P)

    def out_side(c):
        ct = c.reshape(J, GL, H, P).transpose(0, 1, 3, 2)
        return (ct[:, :, :, None, :] * eye[None, :, None, :, None]).reshape(J, GL * P, GL * H)

    wb = jnp.concatenate([in_side(b_re), in_side(b_im)], axis=2).astype(BF16)
    wc = jnp.concatenate([out_side(ct_re), -out_side(ct_im)], axis=1).astype(BF16)
    return wb, wc


def kernel(x, norm_g, final_norm_g, sb_w_qkv, sb_w_o, sg_w_in, sg_norm_g, sg_w_s, sg_b, sg_w_o,
           ssm_w_in, ssm_lam_re, ssm_lam_im, ssm_log_dt, ssm_b_re, ssm_b_im, ssm_c_re, ssm_c_im,
           ssm_d, ssm_w_glu, ffn_w_up, ffn_conv_w, ffn_conv_b, ffn_w_down):
    D = D_MODEL
    F = D_FF
    for i in range(DEPTH):
        mixer = i % N_MIXERS
        j = i // N_MIXERS
        g1 = norm_g[i, 0].reshape(1, D)
        g2 = norm_g[i, 1].reshape(1, D)
        if mixer == 0:
            w = sb_w_qkv[j]
            q, kt, v = _qkv_call(x, g1, w[:, :D].astype(BF16), w[:, D:2 * D].T.astype(BF16),
                                 w[:, 2 * D:].astype(BF16))
            m = _sb_call(q, kt, v)
            wm = sb_w_o[j].astype(BF16)
            glu = False
        elif mixer == 1:
            m = _sg_call(x, g1, sg_w_in[j].astype(BF16), sg_norm_g[j].reshape(1, D), sg_w_s[j],
                         sg_b[j].T)
            wm = sg_w_o[j].astype(BF16)
            glu = False
        else:
            ar, ai, ct_re, ct_im = _ssm_prep_call(ssm_lam_re[j], ssm_lam_im[j], ssm_log_dt[j],
                                                  ssm_c_re[j], ssm_c_im[j])
            wb, wc = _ssm_block_weights(ssm_b_re[j], ssm_b_im[j], ct_re, ct_im)
            m = _ssm_call(x, g1, ssm_w_in[j].astype(BF16), wb, wc,
                          ar.reshape(SSM_SLABS, SSM_SLAB_STATES), ai.reshape(SSM_SLABS, SSM_SLAB_STATES),
                          ssm_d[j].reshape(SSM_SLABS, LANES))
            wm = ssm_w_glu[j].astype(BF16)
            glu = True
        wu = ffn_w_up[i]
        cw = ffn_conv_w[i]
        cb = ffn_conv_b[i].reshape(1, 2 * F)
        fg = final_norm_g.reshape(1, D) if i == DEPTH - 1 else None
        x = _ffn_call(x, m, wm, g2, wu[:, :F].astype(BF16), wu[:, F:].astype(BF16),
                      cw[:, :F], cw[:, F:], cb[:, :F], cb[:, F:], ffn_w_down[i].astype(BF16), fg,
                      glu=glu)
    return x
```

```python
import functools
import math

import jax
import jax.numpy as jnp
from jax import lax
from jax.experimental import pallas as pl
from jax.experimental.pallas import tpu as pltpu

F32 = jnp.float32
BF16 = jnp.bfloat16

D_MODEL = 1024
EPS = 1e-6
LOG2E = math.log2(math.e)
SB_HEAD_DIM = 64
SB_PAIR = 2 * SB_HEAD_DIM
SB_PAIRS = D_MODEL // SB_PAIR
Q_BLOCK = 128
MASKED_LOGIT = -1e30
EXIT_LOGIT = -110.0
CHUNK = 128
SG_GROUPS = 8
SG_HEAD_DIM = D_MODEL // SG_GROUPS
SSM_GROUP = 16
SSM_GROUPS = 64
SSM_STATE = 64
SSM_SLABS = 8
SSM_SLAB_GROUPS = SSM_GROUPS // SSM_SLABS
SSM_SLAB_STATES = SSM_SLAB_GROUPS * SSM_STATE
D_FF = 2816
CONV_K = 3
DEPTH = 4
N_MIXERS = 3

V7X_VMEM_LIMIT_BYTES = 56 * 1024 * 1024
SUBLANES = 8
LANES = 128

FFN_ROWS = 512
FFN_FC = 256
QKV_ROWS = 512
SB_QBLOCKS_PER_STEP = 4
SG_ROWS = 512
SSM_STEPS = 128


def _rms(x, g):
    return x * lax.rsqrt(jnp.mean(x * x, axis=-1, keepdims=True) + EPS) * g


def _sigmoid(x):
    return 1.0 / (1.0 + jnp.exp(-x))


def _gelu(x):
    c = math.sqrt(2.0 / math.pi)
    return 0.5 * x * (1.0 + jnp.tanh(c * (x + 0.044715 * (x * x * x))))


def _resident(shape):
    nd = len(shape)
    return pl.BlockSpec(shape, lambda *_: (0,) * nd, pipeline_mode=pl.Buffered(1))


def _params(n_axes):
    return pltpu.CompilerParams(
        dimension_semantics=("arbitrary",) * n_axes,
        vmem_limit_bytes=V7X_VMEM_LIMIT_BYTES,
    )


def _ffn_kernel(*refs, glu, final, tl, fc, n_chunks):
    if final:
        (x_ref, m_ref, wm_ref, g_ref, wa_ref, wg_ref, cwa_ref, cwg_ref, cba_ref, cbg_ref, wd_ref,
         fg_ref, o_ref, xn_s, hsa, hsg, cara, carg, act) = refs
    else:
        (x_ref, m_ref, wm_ref, g_ref, wa_ref, wg_ref, cwa_ref, cwg_ref, cba_ref, cbg_ref, wd_ref,
         o_ref, xn_s, hsa, hsg, cara, carg, act) = refs
        fg_ref = None

    @pl.when(pl.program_id(1) == 0)
    def _():
        cara[...] = jnp.zeros_like(cara)
        carg[...] = jnp.zeros_like(carg)

    mm = jnp.dot(m_ref[...], wm_ref[...], preferred_element_type=F32)
    if glu:
        mix = mm[:, :D_MODEL] * _sigmoid(mm[:, D_MODEL:])
    else:
        mix = mm
    x1 = x_ref[...] + mix
    o_ref[...] = x1
    xn_s[...] = _rms(x1, g_ref[...]).astype(BF16)

    def conv(hs, car, cw_ref, cb_ref, h, cols):
        hs[0:SUBLANES, :] = car[:, cols]
        hs[SUBLANES:SUBLANES + tl, :] = h
        car[:, cols] = h[tl - SUBLANES:tl, :]
        w = cw_ref[:, cols]
        y = (w[2:3, :] * hs[SUBLANES:SUBLANES + tl, :]
             + w[1:2, :] * hs[SUBLANES - 1:SUBLANES - 1 + tl, :]
             + w[0:1, :] * hs[SUBLANES - 2:SUBLANES - 2 + tl, :])
        return y + cb_ref[:, cols]

    for c in range(n_chunks):
        cols = slice(c * fc, (c + 1) * fc)
        xn = xn_s[...]
        ha = jnp.dot(xn, wa_ref[:, cols], preferred_element_type=F32)
        hg = jnp.dot(xn, wg_ref[:, cols], preferred_element_type=F32)
        ya = conv(hsa, cara, cwa_ref, cba_ref, ha, cols)
        yg = conv(hsg, carg, cwg_ref, cbg_ref, hg, cols)
        act[:, cols] = (yg * _sigmoid(yg) * ya).astype(BF16)

    head = (n_chunks - 1) * fc
    out = (o_ref[...] + jnp.dot(act[:, :head], wd_ref[:head, :], preferred_element_type=F32)
           + jnp.dot(act[:, head:], wd_ref[head:, :], preferred_element_type=F32))
    if final:
        out = _rms(out, fg_ref[...])
    o_ref[...] = out


def _ffn_call(x, m, wm, g, wa, wg, cwa, cwg, cba, cbg, wd, fg, *, glu, tl=FFN_ROWS, fc=FFN_FC):
    B, L, D = x.shape
    F = wa.shape[1]
    dm = m.shape[2]
    final = fg is not None
    n_chunks = F // fc
    row = lambda b, i: (b, i, 0)
    in_specs = [
        pl.BlockSpec((None, tl, D), row),
        pl.BlockSpec((None, tl, dm), row),
        _resident(wm.shape), _resident(g.shape), _resident(wa.shape), _resident(wg.shape),
        _resident(cwa.shape), _resident(cwg.shape), _resident(cba.shape), _resident(cbg.shape),
        _resident(wd.shape),
    ]
    args = [x, m, wm, g, wa, wg, cwa, cwg, cba, cbg, wd]
    if final:
        in_specs.append(_resident(fg.shape))
        args.append(fg)
    return pl.pallas_call(
        functools.partial(_ffn_kernel, glu=glu, final=final, tl=tl, fc=fc, n_chunks=n_chunks),
        grid=(B, L // tl),
        in_specs=in_specs,
        out_specs=pl.BlockSpec((None, tl, D), row),
        out_shape=jax.ShapeDtypeStruct((B, L, D), F32),
        scratch_shapes=[
            pltpu.VMEM((tl, D), BF16),
            pltpu.VMEM((tl + SUBLANES, fc), F32),
            pltpu.VMEM((tl + SUBLANES, fc), F32),
            pltpu.VMEM((SUBLANES, F), F32),
            pltpu.VMEM((SUBLANES, F), F32),
            pltpu.VMEM((tl, F), BF16),
        ],
        compiler_params=_params(2),
        name="ffn",
    )(*args)


def _qkv_kernel(x_ref, g_ref, wq_ref, wkt_ref, wv_ref, q_ref, kt_ref, v_ref, *, tl):
    xn = _rms(x_ref[...], g_ref[...]).astype(BF16)
    q = jnp.dot(xn, wq_ref[...], preferred_element_type=F32) * (SB_HEAD_DIM ** -0.5)
    q_ref[...] = q.astype(BF16)
    v_ref[...] = jnp.dot(xn, wv_ref[...], preferred_element_type=F32).astype(BF16)
    kt = lax.dot_general(wkt_ref[...], xn, (((1,), (1,)), ((), ())), preferred_element_type=F32)
    kt = kt.astype(BF16)
    for t in range(tl // Q_BLOCK):
        kt_ref[t] = kt[:, t * Q_BLOCK:(t + 1) * Q_BLOCK]


def _qkv_call(x, g, wq, wkt, wv, *, tl=QKV_ROWS):
    B, L, D = x.shape
    nk = tl // Q_BLOCK
    row = lambda b, i: (b, i, 0)
    return pl.pallas_call(
        functools.partial(_qkv_kernel, tl=tl),
        grid=(B, L // tl),
        in_specs=[pl.BlockSpec((None, tl, D), row), _resident(g.shape), _resident(wq.shape),
                  _resident(wkt.shape), _resident(wv.shape)],
        out_specs=[pl.BlockSpec((None, tl, D), row),
                   pl.BlockSpec((None, nk, D, Q_BLOCK), lambda b, i: (b, i, 0, 0)),
                   pl.BlockSpec((None, tl, D), row)],
        out_shape=[jax.ShapeDtypeStruct((B, L, D), BF16),
                   jax.ShapeDtypeStruct((B, L // Q_BLOCK, D, Q_BLOCK), BF16),
                   jax.ShapeDtypeStruct((B, L, D), BF16)],
        compiler_params=_params(2),
        name="sb_qkv",
    )(x, g, wq, wkt, wv)


def _sb_kernel(q_ref, kt_ref, v_ref, o_ref, acc, csum, kbd, vbd, *, nk, qb):
    step = pl.program_id(1)
    W = 2 * Q_BLOCK
    pairs = range(SB_PAIRS)
    col = lambda hp: slice(hp * SB_PAIR, (hp + 1) * SB_PAIR)

    @pl.when(step == 0)
    def _():
        d_row = lax.broadcasted_iota(jnp.int32, (SB_PAIR, Q_BLOCK), 0)
        d_col = lax.broadcasted_iota(jnp.int32, (Q_BLOCK, SB_PAIR), 1)

        def build(jb, carry):
            k0 = pl.multiple_of(jb * Q_BLOCK, Q_BLOCK)
            for hp in pairs:
                kt = kt_ref[jb, col(hp), :]
                zk = jnp.zeros_like(kt)
                kbd[jb, hp] = jnp.concatenate([jnp.where(d_row < SB_HEAD_DIM, kt, zk),
                                               jnp.where(d_row >= SB_HEAD_DIM, kt, zk)], axis=1)
                v2 = v_ref[pl.ds(k0, Q_BLOCK), col(hp)]
                zv = jnp.zeros_like(v2)
                vbd[jb, hp] = jnp.concatenate([jnp.where(d_col < SB_HEAD_DIM, v2, zv),
                                               jnp.where(d_col >= SB_HEAD_DIM, v2, zv)], axis=0)
            return carry

        lax.fori_loop(0, nk, build, 0)

    r = lax.broadcasted_iota(jnp.int32, (W, W), 0)
    c = lax.broadcasted_iota(jnp.int32, (W, W), 1)
    same = (r < Q_BLOCK) == (c < Q_BLOCK)
    tri = jnp.where(same & (r > c), 1.0, 0.0).astype(BF16)
    ones = jnp.where(same, 1.0, 0.0).astype(BF16)
    tri_ones = jnp.concatenate([tri, ones], axis=1)

    t_row = lax.broadcasted_iota(jnp.int32, (Q_BLOCK, W), 0)
    s_col = lax.broadcasted_iota(jnp.int32, (Q_BLOCK, W), 1)
    s_col = jnp.where(s_col >= Q_BLOCK, s_col - Q_BLOCK, s_col)
    causal = s_col < t_row

    def query_block(qq, carry):
        i = step * qb + qq
        rows = pl.ds(pl.multiple_of(qq * Q_BLOCK, Q_BLOCK), Q_BLOCK)
        acc[...] = jnp.zeros_like(acc)
        csum[...] = jnp.zeros_like(csum)

        def scores(j, hp, masked):
            z = jnp.dot(q_ref[rows, col(hp)], kbd[j, hp], preferred_element_type=F32)
            ls = jnp.minimum(z, 0.0) - jnp.log(1.0 + jnp.exp2(jnp.abs(z) * -LOG2E))
            l1 = ls - z
            if masked:
                l1 = jnp.where(causal, l1, 0.0)
                ls = jnp.where(causal, ls, MASKED_LOGIT)
            return ls, l1.astype(BF16)

        def weights(hp, ls, l1):
            es = jnp.dot(l1, tri_ones, preferred_element_type=F32)
            later = csum[hp]
            w = jnp.exp(ls + es[:, :W] + later)
            later = later + es[:, W:]
            csum[hp] = later
            return w.astype(BF16), later

        def tile(j, masked):
            staged = [scores(j, hp, masked) for hp in pairs]
            wts = []
            top = None
            for hp in pairs:
                w, later = weights(hp, *staged[hp])
                top = later if top is None else jnp.maximum(top, later)
                wts.append(w)
            for hp in pairs:
                acc[hp] += jnp.dot(wts[hp], vbd[j, hp], preferred_element_type=F32)
            return (jnp.max(top) >= EXIT_LOGIT).astype(jnp.int32)

        def cond(c):
            j, live = c
            return jnp.logical_and(j >= 0, live > 0)

        def body(c):
            j, _ = c
            return j - 1, tile(j, False)

        lax.while_loop(cond, body, (i - 1, tile(i, True)))

        for hp in pairs:
            o_ref[rows, col(hp)] = acc[hp].astype(BF16)
        return carry

    lax.fori_loop(0, qb, query_block, 0)


def _sb_call(q, kt, v, *, qb=SB_QBLOCKS_PER_STEP):
    B, L, D = q.shape
    nq = L // Q_BLOCK
    rows = qb * Q_BLOCK
    return pl.pallas_call(
        functools.partial(_sb_kernel, nk=nq, qb=qb),
        grid=(B, nq // qb),
        in_specs=[pl.BlockSpec((None, rows, D), lambda b, i: (b, i, 0)),
                  pl.BlockSpec((None, nq, D, Q_BLOCK), lambda b, i: (b, 0, 0, 0)),
                  pl.BlockSpec((None, L, D), lambda b, i: (b, 0, 0))],
        out_specs=pl.BlockSpec((None, rows, D), lambda b, i: (b, i, 0)),
        out_shape=jax.ShapeDtypeStruct((B, L, D), BF16),
        scratch_shapes=[pltpu.VMEM((SB_PAIRS, Q_BLOCK, SB_PAIR), F32),
                        pltpu.VMEM((SB_PAIRS, Q_BLOCK, 2 * Q_BLOCK), F32),
                        pltpu.VMEM((nq, SB_PAIRS, SB_PAIR, 2 * Q_BLOCK), BF16),
                        pltpu.VMEM((nq, SB_PAIRS, 2 * Q_BLOCK, SB_PAIR), BF16)],
        compiler_params=_params(2),
        name="sb_attn",
    )(q, kt, v)


def _sg_kernel(x_ref, g_ref, win_ref, vg_ref, ws_ref, bst_ref, m_ref, u_s, v_s, *, tl):
    xn = _rms(x_ref[...], g_ref[...]).astype(BF16)
    h = _gelu(jnp.dot(xn, win_ref[...], preferred_element_type=F32))
    u_s[...] = h[:, :D_MODEL]
    v_s[...] = _rms(h[:, D_MODEL:], vg_ref[...]).astype(BF16)
    r = lax.broadcasted_iota(jnp.int32, (CHUNK, CHUNK), 0)
    c = lax.broadcasted_iota(jnp.int32, (CHUNK, CHUNK), 1)
    for g in range(SG_GROUPS):
        cols = slice(g * SG_HEAD_DIM, (g + 1) * SG_HEAD_DIM)
        w = jnp.where(r >= c, ws_ref[g], 0.0).astype(BF16)
        bias = jnp.broadcast_to(bst_ref[:, g:g + 1], (CHUNK, SG_HEAD_DIM))
        for ch in range(tl // CHUNK):
            rows = slice(ch * CHUNK, (ch + 1) * CHUNK)
            sv = jnp.dot(w, v_s[rows, cols], preferred_element_type=F32) + bias
            m_ref[rows, cols] = (u_s[rows, cols] * sv).astype(BF16)


def _sg_call(x, g, win, vg, ws, bst, *, tl=SG_ROWS):
    B, L, D = x.shape
    row = lambda b, i: (b, i, 0)
    return pl.pallas_call(
        functools.partial(_sg_kernel, tl=tl),
        grid=(B, L // tl),
        in_specs=[pl.BlockSpec((None, tl, D), row), _resident(g.shape), _resident(win.shape),
                  _resident(vg.shape), _resident(ws.shape), _resident(bst.shape)],
        out_specs=pl.BlockSpec((None, tl, D), row),
        out_shape=jax.ShapeDtypeStruct((B, L, D), BF16),
        scratch_shapes=[pltpu.VMEM((tl, D), F32), pltpu.VMEM((tl, D), BF16)],
        compiler_params=_params(2),
        name="sg_gate",
    )(x, g, win, vg, ws, bst)


def _ssm_prep_kernel(lre_ref, lim_ref, ldt_ref, cre_ref, cim_ref, ar_ref, ai_ref, ctr_ref, cti_ref):
    lr = jnp.minimum(lre_ref[...], -1e-4)
    li = lim_ref[...]
    dt = jnp.exp(ldt_ref[...])
    mag = jnp.exp(dt * lr)
    ar = mag * jnp.cos(dt * li)
    ai = mag * jnp.sin(dt * li)
    den = lr * lr + li * li
    cr = ((ar - 1.0) * lr + ai * li) / den
    ci = (ai * lr - (ar - 1.0) * li) / den
    ar_ref[...] = ar
    ai_ref[...] = ai
    cre = cre_ref[...]
    cim = cim_ref[...]
    cr3 = cr[:, None, :]
    ci3 = ci[:, None, :]
    ctr_ref[...] = cre * cr3 - cim * ci3
    cti_ref[...] = cre * ci3 + cim * cr3


def _ssm_prep_call(lam_re, lam_im, log_dt, c_re, c_im):
    G, P = lam_re.shape
    H = c_re.shape[1]
    return pl.pallas_call(
        _ssm_prep_kernel,
        out_shape=[jax.ShapeDtypeStruct((G, P), F32), jax.ShapeDtypeStruct((G, P), F32),
                   jax.ShapeDtypeStruct((G, H, P), F32), jax.ShapeDtypeStruct((G, H, P), F32)],
        name="ssm_prep",
    )(lam_re, lam_im, log_dt.reshape(G, 1), c_re, c_im)


def _ssm_kernel(x_ref, g_ref, win_ref, wb_ref, wc_ref, ar_ref, ai_ref, d_ref, o_ref,
                u_lb, bu, y_lb, st, *, tl, nb):
    S = SSM_SLAB_STATES

    @pl.when(pl.program_id(0) == 0)
    def _():
        st[...] = jnp.zeros_like(st)

    for b in range(nb):
        xn = _rms(x_ref[b], g_ref[...]).astype(BF16)
        u = jnp.dot(xn, win_ref[...], preferred_element_type=F32)
        for j in range(SSM_SLABS):
            u_lb[j, pl.ds(b, tl, stride=nb), :] = u[:, j * LANES:(j + 1) * LANES]

    for j in range(SSM_SLABS):
        uj = u_lb[j]
        bu[...] = jnp.dot(uj.astype(BF16), wb_ref[j], preferred_element_type=F32)
        ar = jnp.broadcast_to(ar_ref[j:j + 1, :], (nb, S))
        ai = jnp.broadcast_to(ai_ref[j:j + 1, :], (nb, S))

        def step(t, carry):
            s_re, s_im = carry
            r0 = pl.multiple_of(t * nb, nb)
            n_re = ar * s_re - ai * s_im + bu[pl.ds(r0, nb), 0:S]
            n_im = ar * s_im + ai * s_re + bu[pl.ds(r0, nb), S:2 * S]
            bu[pl.ds(r0, nb), 0:S] = n_re
            bu[pl.ds(r0, nb), S:2 * S] = n_im
            return n_re, n_im

        s_re, s_im = lax.fori_loop(0, tl, step, (st[j, :, 0:S], st[j, :, S:2 * S]), unroll=True)
        st[j, :, 0:S] = s_re
        st[j, :, S:2 * S] = s_im
        y = jnp.dot(bu[...].astype(BF16), wc_ref[j], preferred_element_type=F32)
        y_lb[j] = _gelu(y + d_ref[j:j + 1, :] * uj)

    for b in range(nb):
        for j in range(SSM_SLABS):
            o_ref[b, :, j * LANES:(j + 1) * LANES] = y_lb[j, pl.ds(b, tl, stride=nb), :].astype(BF16)


def _ssm_call(x, g, win, wb, wc, ar, ai, d, *, tl=SSM_STEPS):
    B, L, D = x.shape
    R = B * tl
    blk = lambda i: (0, i, 0)
    return pl.pallas_call(
        functools.partial(_ssm_kernel, tl=tl, nb=B),
        grid=(L // tl,),
        in_specs=[pl.BlockSpec((B, tl, D), blk), _resident(g.shape), _resident(win.shape),
                  _resident(wb.shape), _resident(wc.shape), _resident(ar.shape), _resident(ai.shape),
                  _resident(d.shape)],
        out_specs=pl.BlockSpec((B, tl, D), blk),
        out_shape=jax.ShapeDtypeStruct((B, L, D), BF16),
        scratch_shapes=[pltpu.VMEM((SSM_SLABS, R, LANES), F32),
                        pltpu.VMEM((R, 2 * SSM_SLAB_STATES), F32),
                        pltpu.VMEM((SSM_SLABS, R, LANES), F32),
                        pltpu.VMEM((SSM_SLABS, B, 2 * SSM_SLAB_STATES), F32)],
        compiler_params=_params(1),
        name="ssm_scan",
    )(x, g, win, wb, wc, ar, ai, d)


def _ssm_block_weights(b_re, b_im, ct_re, ct_im):
    J, GL, P, H = SSM_SLABS, SSM_SLAB_GROUPS, SSM_STATE, SSM_GROUP
    eye = jnp.eye(GL, dtype=F32)

    def in_side(b):
        bt = b.reshape(J, GL, P, H).transpose(0, 1, 3, 2)
        return (bt[:, :, :, None, :] * eye[None, :, None, :, None]).reshape(J, GL * H, GL * P)

    def out_side(c):
        ct = c.reshape(J, GL, H, P).transpose(0, 1, 3, 2)
        return (ct[:, :, :, None, :] * eye[None, :, None, :, None]).reshape(J, GL * P, GL * H)

    wb = jnp.concatenate([in_side(b_re), in_side(b_im)], axis=2).astype(BF16)
    wc = jnp.concatenate([out_side(ct_re), -out_side(ct_im)], axis=1).astype(BF16)
    return wb, wc


def kernel(x, norm_g, final_norm_g, sb_w_qkv, sb_w_o, sg_w_in, sg_norm_g, sg_w_s, sg_b, sg_w_o,
           ssm_w_in, ssm_lam_re, ssm_lam_im, ssm_log_dt, ssm_b_re, ssm_b_im, ssm_c_re, ssm_c_im,
           ssm_d, ssm_w_glu, ffn_w_up, ffn_conv_w, ffn_conv_b, ffn_w_down):
    D = D_MODEL
    F = D_FF
    for i in range(DEPTH):
        mixer = i % N_MIXERS
        j = i // N_MIXERS
        g1 = norm_g[i, 0].reshape(1, D)
        g2 = norm_g[i, 1].reshape(1, D)
        if mixer == 0:
            w = sb_w_qkv[j]
            q, kt, v = _qkv_call(x, g1, w[:, :D].astype(BF16), w[:, D:2 * D].T.astype(BF16),
                                 w[:, 2 * D:].astype(BF16))
            m = _sb_call(q, kt, v)
            wm = sb_w_o[j].astype(BF16)
            glu = False
        elif mixer == 1:
            m = _sg_call(x, g1, sg_w_in[j].astype(BF16), sg_norm_g[j].reshape(1, D), sg_w_s[j],
                         sg_b[j].T)
            wm = sg_w_o[j].astype(BF16)
            glu = False
        else:
            ar, ai, ct_re, ct_im = _ssm_prep_call(ssm_lam_re[j], ssm_lam_im[j], ssm_log_dt[j],
                                                  ssm_c_re[j], ssm_c_im[j])
            wb, wc = _ssm_block_weights(ssm_b_re[j], ssm_b_im[j], ct_re, ct_im)
            m = _ssm_call(x, g1, ssm_w_in[j].astype(BF16), wb, wc,
                          ar.reshape(SSM_SLABS, SSM_SLAB_STATES), ai.reshape(SSM_SLABS, SSM_SLAB_STATES),
                          ssm_d[j].reshape(SSM_SLABS, LANES))
            wm = ssm_w_glu[j].astype(BF16)
            glu = True
        wu = ffn_w_up[i]
        cw = ffn_conv_w[i]
        cb = ffn_conv_b[i].reshape(1, 2 * F)
        fg = final_norm_g.reshape(1, D) if i == DEPTH - 1 else None
        x = _ffn_call(x, m, wm, g2, wu[:, :F].astype(BF16), wu[:, F:].astype(BF16),
                      cw[:, :F], cw[:, F:], cb[:, :F], cb[:, F:], ffn_w_down[i].astype(BF16), fg,
                      glu=glu)
    return x
```

```python
import functools
import math

import jax
import jax.numpy as jnp
from jax import lax
from jax.experimental import pallas as pl
from jax.experimental.pallas import tpu as pltpu

F32 = jnp.float32
BF16 = jnp.bfloat16

D_MODEL = 1024
EPS = 1e-6
LOG2E = math.log2(math.e)
SB_HEAD_DIM = 64
SB_PAIR = 2 * SB_HEAD_DIM
SB_PAIRS = D_MODEL // SB_PAIR
Q_BLOCK = 128
MASKED_LOGIT = -1e30
EXIT_LOGIT = -110.0
CHUNK = 128
SG_GROUPS = 8
SG_HEAD_DIM = D_MODEL // SG_GROUPS
SSM_GROUP = 16
SSM_GROUPS = 64
SSM_STATE = 64
SSM_SLABS = 8
SSM_SLAB_GROUPS = SSM_GROUPS // SSM_SLABS
SSM_SLAB_STATES = SSM_SLAB_GROUPS * SSM_STATE
D_FF = 2816
CONV_K = 3
DEPTH = 4
N_MIXERS = 3

V7X_VMEM_LIMIT_BYTES = 56 * 1024 * 1024
SUBLANES = 8
LANES = 128

FFN_ROWS = 512
FFN_FC = 256
QKV_ROWS = 1024
SB_QBLOCKS_PER_STEP = 4
SB_FUSED_BLOCKS = 3
SG_ROWS = 512
SSM_STEPS = 128


def _rms(x, g):
    return x * lax.rsqrt(jnp.mean(x * x, axis=-1, keepdims=True) + EPS) * g


def _sigmoid(x):
    return 1.0 / (1.0 + jnp.exp(-x))


def _gelu(x):
    c = math.sqrt(2.0 / math.pi)
    return 0.5 * x * (1.0 + jnp.tanh(c * (x + 0.044715 * (x * x * x))))


def _resident(shape):
    nd = len(shape)
    return pl.BlockSpec(shape, lambda *_: (0,) * nd, pipeline_mode=pl.Buffered(1))


def _params(n_axes):
    return pltpu.CompilerParams(
        dimension_semantics=("arbitrary",) * n_axes,
        vmem_limit_bytes=V7X_VMEM_LIMIT_BYTES,
    )


def _ffn_kernel(*refs, glu, final, tl, fc, n_chunks):
    if final:
        (x_ref, m_ref, wm_ref, g_ref, wa_ref, wg_ref, cwa_ref, cwg_ref, cba_ref, cbg_ref, wd_ref,
         fg_ref, o_ref, xn_s, hsa, hsg, cara, carg, act) = refs
    else:
        (x_ref, m_ref, wm_ref, g_ref, wa_ref, wg_ref, cwa_ref, cwg_ref, cba_ref, cbg_ref, wd_ref,
         o_ref, xn_s, hsa, hsg, cara, carg, act) = refs
        fg_ref = None

    @pl.when(pl.program_id(1) == 0)
    def _():
        cara[...] = jnp.zeros_like(cara)
        carg[...] = jnp.zeros_like(carg)

    mm = jnp.dot(m_ref[...], wm_ref[...], preferred_element_type=F32)
    if glu:
        mix = mm[:, :D_MODEL] * _sigmoid(mm[:, D_MODEL:])
    else:
        mix = mm
    x1 = x_ref[...] + mix
    o_ref[...] = x1
    xn_s[...] = _rms(x1, g_ref[...]).astype(BF16)

    def conv(hs, car, cw_ref, cb_ref, h, cols):
        hs[0:SUBLANES, :] = car[:, cols]
        hs[SUBLANES:SUBLANES + tl, :] = h
        car[:, cols] = h[tl - SUBLANES:tl, :]
        w = cw_ref[:, cols]
        y = (w[2:3, :] * hs[SUBLANES:SUBLANES + tl, :]
             + w[1:2, :] * hs[SUBLANES - 1:SUBLANES - 1 + tl, :]
             + w[0:1, :] * hs[SUBLANES - 2:SUBLANES - 2 + tl, :])
        return y + cb_ref[:, cols]

    for c in range(n_chunks):
        cols = slice(c * fc, (c + 1) * fc)
        xn = xn_s[...]
        ha = jnp.dot(xn, wa_ref[:, cols], preferred_element_type=F32)
        hg = jnp.dot(xn, wg_ref[:, cols], preferred_element_type=F32)
        ya = conv(hsa, cara, cwa_ref, cba_ref, ha, cols)
        yg = conv(hsg, carg, cwg_ref, cbg_ref, hg, cols)
        act[:, cols] = (yg * _sigmoid(yg) * ya).astype(BF16)

    head = (n_chunks - 1) * fc
    out = (o_ref[...] + jnp.dot(act[:, :head], wd_ref[:head, :], preferred_element_type=F32)
           + jnp.dot(act[:, head:], wd_ref[head:, :], preferred_element_type=F32))
    if final:
        out = _rms(out, fg_ref[...])
    o_ref[...] = out


def _ffn_call(x, m, wm, g, wa, wg, cwa, cwg, cba, cbg, wd, fg, *, glu, tl=FFN_ROWS, fc=FFN_FC):
    B, L, D = x.shape
    F = wa.shape[1]
    dm = m.shape[2]
    final = fg is not None
    n_chunks = F // fc
    row = lambda b, i: (b, i, 0)
    in_specs = [
        pl.BlockSpec((None, tl, D), row),
        pl.BlockSpec((None, tl, dm), row),
        _resident(wm.shape), _resident(g.shape), _resident(wa.shape), _resident(wg.shape),
        _resident(cwa.shape), _resident(cwg.shape), _resident(cba.shape), _resident(cbg.shape),
        _resident(wd.shape),
    ]
    args = [x, m, wm, g, wa, wg, cwa, cwg, cba, cbg, wd]
    if final:
        in_specs.append(_resident(fg.shape))
        args.append(fg)
    return pl.pallas_call(
        functools.partial(_ffn_kernel, glu=glu, final=final, tl=tl, fc=fc, n_chunks=n_chunks),
        grid=(B, L // tl),
        in_specs=in_specs,
        out_specs=pl.BlockSpec((None, tl, D), row),
        out_shape=jax.ShapeDtypeStruct((B, L, D), F32),
        scratch_shapes=[
            pltpu.VMEM((tl, D), BF16),
            pltpu.VMEM((tl + SUBLANES, fc), F32),
            pltpu.VMEM((tl + SUBLANES, fc), F32),
            pltpu.VMEM((SUBLANES, F), F32),
            pltpu.VMEM((SUBLANES, F), F32),
            pltpu.VMEM((tl, F), BF16),
        ],
        compiler_params=_params(2),
        name="ffn",
    )(*args)


def _qkv_kernel(x_ref, g_ref, wq_ref, wkt_ref, wv_ref, q_ref, kt_ref, v_ref, *, tl):
    xn = _rms(x_ref[...], g_ref[...]).astype(BF16)
    q = jnp.dot(xn, wq_ref[...], preferred_element_type=F32) * (SB_HEAD_DIM ** -0.5)
    q_ref[...] = q.astype(BF16)
    v_ref[...] = jnp.dot(xn, wv_ref[...], preferred_element_type=F32).astype(BF16)
    kt = lax.dot_general(wkt_ref[...], xn, (((1,), (1,)), ((), ())), preferred_element_type=F32)
    kt = kt.astype(BF16)
    for t in range(tl // Q_BLOCK):
        kt_ref[t] = kt[:, t * Q_BLOCK:(t + 1) * Q_BLOCK]


def _qkv_call(x, g, wq, wkt, wv, *, tl=QKV_ROWS):
    B, L, D = x.shape
    nk = tl // Q_BLOCK
    row = lambda b, i: (b, i, 0)
    return pl.pallas_call(
        functools.partial(_qkv_kernel, tl=tl),
        grid=(B, L // tl),
        in_specs=[pl.BlockSpec((None, tl, D), row), _resident(g.shape), _resident(wq.shape),
                  _resident(wkt.shape), _resident(wv.shape)],
        out_specs=[pl.BlockSpec((None, tl, D), row),
                   pl.BlockSpec((None, nk, D, Q_BLOCK), lambda b, i: (b, i, 0, 0)),
                   pl.BlockSpec((None, tl, D), row)],
        out_shape=[jax.ShapeDtypeStruct((B, L, D), BF16),
                   jax.ShapeDtypeStruct((B, L // Q_BLOCK, D, Q_BLOCK), BF16),
                   jax.ShapeDtypeStruct((B, L, D), BF16)],
        compiler_params=_params(2),
        name="sb_qkv",
    )(x, g, wq, wkt, wv)


def _sb_kernel(q_ref, kt_ref, v_ref, o_ref, acc, csum, kbd, vbd, *, nk, qb):
    step = pl.program_id(1)
    W = 2 * Q_BLOCK
    pairs = range(SB_PAIRS)
    col = lambda hp: slice(hp * SB_PAIR, (hp + 1) * SB_PAIR)

    @pl.when(step == 0)
    def _():
        d_row = lax.broadcasted_iota(jnp.int32, (SB_PAIR, Q_BLOCK), 0)
        d_col = lax.broadcasted_iota(jnp.int32, (Q_BLOCK, SB_PAIR), 1)

        def build(jb, carry):
            k0 = pl.multiple_of(jb * Q_BLOCK, Q_BLOCK)
            for hp in pairs:
                kt = kt_ref[jb, col(hp), :]
                zk = jnp.zeros_like(kt)
                kbd[jb, hp] = jnp.concatenate([jnp.where(d_row < SB_HEAD_DIM, kt, zk),
                                               jnp.where(d_row >= SB_HEAD_DIM, kt, zk)], axis=1)
                v2 = v_ref[pl.ds(k0, Q_BLOCK), col(hp)]
                zv = jnp.zeros_like(v2)
                vbd[jb, hp] = jnp.concatenate([jnp.where(d_col < SB_HEAD_DIM, v2, zv),
                                               jnp.where(d_col >= SB_HEAD_DIM, v2, zv)], axis=0)
            return carry

        lax.fori_loop(0, nk, build, 0)

    r = lax.broadcasted_iota(jnp.int32, (W, W), 0)
    c = lax.broadcasted_iota(jnp.int32, (W, W), 1)
    same = (r < Q_BLOCK) == (c < Q_BLOCK)
    tri = jnp.where(same & (r > c), 1.0, 0.0).astype(BF16)
    ones = jnp.where(same, 1.0, 0.0).astype(BF16)
    tri_ones = jnp.concatenate([tri, ones], axis=1)

    t_row = lax.broadcasted_iota(jnp.int32, (Q_BLOCK, W), 0)
    s_col = lax.broadcasted_iota(jnp.int32, (Q_BLOCK, W), 1)
    s_col = jnp.where(s_col >= Q_BLOCK, s_col - Q_BLOCK, s_col)
    causal = s_col < t_row

    def query_block(qq, carry):
        i = step * qb + qq
        rows = pl.ds(pl.multiple_of(qq * Q_BLOCK, Q_BLOCK), Q_BLOCK)
        acc[...] = jnp.zeros_like(acc)
        csum[...] = jnp.zeros_like(csum)

        def scores(j, hp, masked):
            z = jnp.dot(q_ref[rows, col(hp)], kbd[j, hp], preferred_element_type=F32)
            ls = jnp.minimum(z, 0.0) - jnp.log(1.0 + jnp.exp2(jnp.abs(z) * -LOG2E))
            l1 = ls - z
            if masked:
                l1 = jnp.where(causal, l1, 0.0)
                ls = jnp.where(causal, ls, MASKED_LOGIT)
            return ls, l1.astype(BF16)

        def weights(hp, ls, l1):
            es = jnp.dot(l1, tri_ones, preferred_element_type=F32)
            later = csum[hp]
            w = jnp.exp(ls + es[:, :W] + later)
            later = later + es[:, W:]
            csum[hp] = later
            return w.astype(BF16), later

        def tiles(js, first_masked):
            staged, wts = {}, {}
            top = [None]

            def stage_scores(k):
                staged[k] = [scores(js[k], hp, first_masked and k == 0) for hp in pairs]

            def stage_weights(k):
                top[0] = None
                wts[k] = []
                for hp in pairs:
                    w, later = weights(hp, *staged[k][hp])
                    top[0] = later if top[0] is None else jnp.maximum(top[0], later)
                    wts[k].append(w)

            def stage_values(k):
                for hp in pairs:
                    acc[hp] += jnp.dot(wts[k][hp], vbd[js[k], hp], preferred_element_type=F32)

            for k in range(len(js) + 2):
                if k < len(js):
                    stage_scores(k)
                if 0 <= k - 1 < len(js):
                    stage_weights(k - 1)
                if 0 <= k - 2 < len(js):
                    stage_values(k - 2)
            return (jnp.max(top[0]) >= EXIT_LOGIT).astype(jnp.int32)

        def cond(c):
            j, live = c
            return jnp.logical_and(j >= 0, live > 0)

        def body(c):
            j, _ = c
            return j - 1, tiles([j], False)

        n0 = SB_FUSED_BLOCKS
        start = lax.cond(i >= n0 - 1,
                         lambda: (i - n0, tiles([i - k for k in range(n0)], True)),
                         lambda: (i - 1, tiles([i], True)))
        lax.while_loop(cond, body, start)

        for hp in pairs:
            o_ref[rows, col(hp)] = acc[hp].astype(BF16)
        return carry

    lax.fori_loop(0, qb, query_block, 0)


def _sb_call(q, kt, v, *, qb=SB_QBLOCKS_PER_STEP):
    B, L, D = q.shape
    nq = L // Q_BLOCK
    rows = qb * Q_BLOCK
    return pl.pallas_call(
        functools.partial(_sb_kernel, nk=nq, qb=qb),
        grid=(B, nq // qb),
        in_specs=[pl.BlockSpec((None, rows, D), lambda b, i: (b, i, 0)),
                  pl.BlockSpec((None, nq, D, Q_BLOCK), lambda b, i: (b, 0, 0, 0)),
                  pl.BlockSpec((None, L, D), lambda b, i: (b, 0, 0))],
        out_specs=pl.BlockSpec((None, rows, D), lambda b, i: (b, i, 0)),
        out_shape=jax.ShapeDtypeStruct((B, L, D), BF16),
        scratch_shapes=[pltpu.VMEM((SB_PAIRS, Q_BLOCK, SB_PAIR), F32),
                        pltpu.VMEM((SB_PAIRS, Q_BLOCK, 2 * Q_BLOCK), F32),
                        pltpu.VMEM((nq, SB_PAIRS, SB_PAIR, 2 * Q_BLOCK), BF16),
                        pltpu.VMEM((nq, SB_PAIRS, 2 * Q_BLOCK, SB_PAIR), BF16)],
        compiler_params=_params(2),
        name="sb_attn",
    )(q, kt, v)


def _sg_kernel(x_ref, g_ref, win_ref, vg_ref, ws_ref, bst_ref, m_ref, u_s, v_s, *, tl):
    xn = _rms(x_ref[...], g_ref[...]).astype(BF16)
    h = _gelu(jnp.dot(xn, win_ref[...], preferred_element_type=F32))
    u_s[...] = h[:, :D_MODEL]
    v_s[...] = _rms(h[:, D_MODEL:], vg_ref[...]).astype(BF16)
    r = lax.broadcasted_iota(jnp.int32, (CHUNK, CHUNK), 0)
    c = lax.broadcasted_iota(jnp.int32, (CHUNK, CHUNK), 1)
    for g in range(SG_GROUPS):
        cols = slice(g * SG_HEAD_DIM, (g + 1) * SG_HEAD_DIM)
        w = jnp.where(r >= c, ws_ref[g], 0.0).astype(BF16)
        bias = jnp.broadcast_to(bst_ref[:, g:g + 1], (CHUNK, SG_HEAD_DIM))
        for ch in range(0, tl // CHUNK, 2):
            r0 = slice(ch * CHUNK, (ch + 1) * CHUNK)
            r1 = slice((ch + 1) * CHUNK, (ch + 2) * CHUNK)
            v2 = jnp.concatenate([v_s[r0, cols], v_s[r1, cols]], axis=1)
            sv = jnp.dot(w, v2, preferred_element_type=F32)
            m_ref[r0, cols] = (u_s[r0, cols] * (sv[:, :SG_HEAD_DIM] + bias)).astype(BF16)
            m_ref[r1, cols] = (u_s[r1, cols] * (sv[:, SG_HEAD_DIM:] + bias)).astype(BF16)


def _sg_call(x, g, win, vg, ws, bst, *, tl=SG_ROWS):
    B, L, D = x.shape
    row = lambda b, i: (b, i, 0)
    return pl.pallas_call(
        functools.partial(_sg_kernel, tl=tl),
        grid=(B, L // tl),
        in_specs=[pl.BlockSpec((None, tl, D), row), _resident(g.shape), _resident(win.shape),
                  _resident(vg.shape), _resident(ws.shape), _resident(bst.shape)],
        out_specs=pl.BlockSpec((None, tl, D), row),
        out_shape=jax.ShapeDtypeStruct((B, L, D), BF16),
        scratch_shapes=[pltpu.VMEM((tl, D), F32), pltpu.VMEM((tl, D), BF16)],
        compiler_params=_params(2),
        name="sg_gate",
    )(x, g, win, vg, ws, bst)


def _ssm_prep_kernel(lre_ref, lim_ref, ldt_ref, cre_ref, cim_ref, ar_ref, ai_ref, ctr_ref, cti_ref):
    lr = jnp.minimum(lre_ref[...], -1e-4)
    li = lim_ref[...]
    dt = jnp.exp(ldt_ref[...])
    mag = jnp.exp(dt * lr)
    ar = mag * jnp.cos(dt * li)
    ai = mag * jnp.sin(dt * li)
    den = lr * lr + li * li
    cr = ((ar - 1.0) * lr + ai * li) / den
    ci = (ai * lr - (ar - 1.0) * li) / den
    ar_ref[...] = ar
    ai_ref[...] = ai
    cre = cre_ref[...]
    cim = cim_ref[...]
    cr3 = cr[:, None, :]
    ci3 = ci[:, None, :]
    ctr_ref[...] = cre * cr3 - cim * ci3
    cti_ref[...] = cre * ci3 + cim * cr3


def _ssm_prep_call(lam_re, lam_im, log_dt, c_re, c_im):
    G, P = lam_re.shape
    H = c_re.shape[1]
    return pl.pallas_call(
        _ssm_prep_kernel,
        out_shape=[jax.ShapeDtypeStruct((G, P), F32), jax.ShapeDtypeStruct((G, P), F32),
                   jax.ShapeDtypeStruct((G, H, P), F32), jax.ShapeDtypeStruct((G, H, P), F32)],
        name="ssm_prep",
    )(lam_re, lam_im, log_dt.reshape(G, 1), c_re, c_im)


def _ssm_kernel(x_ref, g_ref, win_ref, wb_ref, wc_ref, ar_ref, ai_ref, d_ref, o_ref,
                u_lb, bu, y_lb, st, *, tl, nb):
    S = SSM_SLAB_STATES

    @pl.when(pl.program_id(0) == 0)
    def _():
        st[...] = jnp.zeros_like(st)

    for b in range(nb):
        xn = _rms(x_ref[b], g_ref[...]).astype(BF16)
        u = jnp.dot(xn, win_ref[...], preferred_element_type=F32)
        for j in range(SSM_SLABS):
            u_lb[j, pl.ds(b, tl, stride=nb), :] = u[:, j * LANES:(j + 1) * LANES]

    for j in range(SSM_SLABS):
        uj = u_lb[j]
        bu[...] = jnp.dot(uj.astype(BF16), wb_ref[j], preferred_element_type=F32)
        ar = jnp.broadcast_to(ar_ref[j:j + 1, :], (nb, S))
        ai = jnp.broadcast_to(ai_ref[j:j + 1, :], (nb, S))

        def step(t, carry):
            s_re, s_im = carry
            r0 = pl.multiple_of(t * nb, nb)
            n_re = ar * s_re - ai * s_im + bu[pl.ds(r0, nb), 0:S]
            n_im = ar * s_im + ai * s_re + bu[pl.ds(r0, nb), S:2 * S]
            bu[pl.ds(r0, nb), 0:S] = n_re
            bu[pl.ds(r0, nb), S:2 * S] = n_im
            return n_re, n_im

        s_re, s_im = lax.fori_loop(0, tl, step, (st[j, :, 0:S], st[j, :, S:2 * S]), unroll=True)
        st[j, :, 0:S] = s_re
        st[j, :, S:2 * S] = s_im
        y = jnp.dot(bu[...].astype(BF16), wc_ref[j], preferred_element_type=F32)
        y_lb[j] = _gelu(y + d_ref[j:j + 1, :] * uj)

    for b in range(nb):
        for j in range(SSM_SLABS):
            o_ref[b, :, j * LANES:(j + 1) * LANES] = y_lb[j, pl.ds(b, tl, stride=nb), :].astype(BF16)


def _ssm_call(x, g, win, wb, wc, ar, ai, d, *, tl=SSM_STEPS):
    B, L, D = x.shape
    R = B * tl
    blk = lambda i: (0, i, 0)
    return pl.pallas_call(
        functools.partial(_ssm_kernel, tl=tl, nb=B),
        grid=(L // tl,),
        in_specs=[pl.BlockSpec((B, tl, D), blk), _resident(g.shape), _resident(win.shape),
                  _resident(wb.shape), _resident(wc.shape), _resident(ar.shape), _resident(ai.shape),
                  _resident(d.shape)],
        out_specs=pl.BlockSpec((B, tl, D), blk),
        out_shape=jax.ShapeDtypeStruct((B, L, D), BF16),
        scratch_shapes=[pltpu.VMEM((SSM_SLABS, R, LANES), F32),
                        pltpu.VMEM((R, 2 * SSM_SLAB_STATES), F32),
                        pltpu.VMEM((SSM_SLABS, R, LANES), F32),
                        pltpu.VMEM((SSM_SLABS, B, 2 * SSM_SLAB_STATES), F32)],
        compiler_params=_params(1),
        name="ssm_scan",
    )(x, g, win, wb, wc, ar, ai, d)


def _ssm_block_weights(b_re, b_im, ct_re, ct_im):
    J, GL, P, H = SSM_SLABS, SSM_SLAB_GROUPS, SSM_STATE, SSM_GROUP
    eye = jnp.eye(GL, dtype=F32)

    def in_side(b):
        bt = b.reshape(J, GL, P, H).transpose(0, 1, 3, 2)
        return (bt[:, :, :, None, :] * eye[None, :, None, :, None]).reshape(J, GL * H, GL * P)

    def out_side(c):
        ct = c.reshape(J, GL, H, P).transpose(0, 1, 3, 2)
        return (ct[:, :, :, None, :] * eye[None, :, None, :, None]).reshape(J, GL * P, GL * H)

    wb = jnp.concatenate([in_side(b_re), in_side(b_im)], axis=2).astype(BF16)
    wc = jnp.concatenate([out_side(ct_re), -out_side(ct_im)], axis=1).astype(BF16)
    return wb, wc


def kernel(x, norm_g, final_norm_g, sb_w_qkv, sb_w_o, sg_w_in, sg_norm_g, sg_w_s, sg_b, sg_w_o,
           ssm_w_in, ssm_lam_re, ssm_lam_im, ssm_log_dt, ssm_b_re, ssm_b_im, ssm_c_re, ssm_c_im,
           ssm_d, ssm_w_glu, ffn_w_up, ffn_conv_w, ffn_conv_b, ffn_w_down):
    D = D_MODEL
    F = D_FF
    for i in range(DEPTH):
        mixer = i % N_MIXERS
        j = i // N_MIXERS
        g1 = norm_g[i, 0].reshape(1, D)
        g2 = norm_g[i, 1].reshape(1, D)
        if mixer == 0:
            w = sb_w_qkv[j]
            q, kt, v = _qkv_call(x, g1, w[:, :D].astype(BF16), w[:, D:2 * D].T.astype(BF16),
                                 w[:, 2 * D:].astype(BF16))
            m = _sb_call(q, kt, v)
            wm = sb_w_o[j].astype(BF16)
            glu = False
        elif mixer == 1:
            m = _sg_call(x, g1, sg_w_in[j].astype(BF16), sg_norm_g[j].reshape(1, D), sg_w_s[j],
                         sg_b[j].T)
            wm = sg_w_o[j].astype(BF16)
            glu = False
        else:
            ar, ai, ct_re, ct_im = _ssm_prep_call(ssm_lam_re[j], ssm_lam_im[j], ssm_log_dt[j],
                                                  ssm_c_re[j], ssm_c_im[j])
            wb, wc = _ssm_block_weights(ssm_b_re[j], ssm_b_im[j], ct_re, ct_im)
            m = _ssm_call(x, g1, ssm_w_in[j].astype(BF16), wb, wc,
                          ar.reshape(SSM_SLABS, SSM_SLAB_STATES), ai.reshape(SSM_SLABS, SSM_SLAB_STATES),
                          ssm_d[j].reshape(SSM_SLABS, LANES))
            wm = ssm_w_glu[j].astype(BF16)
            glu = True
        wu = ffn_w_up[i]
        cw = ffn_conv_w[i]
        cb = ffn_conv_b[i].reshape(1, 2 * F)
        fg = final_norm_g.reshape(1, D) if i == DEPTH - 1 else None
        x = _ffn_call(x, m, wm, g2, wu[:, :F].astype(BF16), wu[:, F:].astype(BF16),
                      cw[:, :F], cw[:, F:], cb[:, :F], cb[:, F:], ffn_w_down[i].astype(BF16), fg,
                      glu=glu)
    return x
```

```python
import functools
import math

import jax
import jax.numpy as jnp
from jax import lax
from jax.experimental import pallas as pl
from jax.experimental.pallas import tpu as pltpu

F32 = jnp.float32
BF16 = jnp.bfloat16

D_MODEL = 1024
EPS = 1e-6
LOG2E = math.log2(math.e)
SB_HEAD_DIM = 64
SB_PAIR = 2 * SB_HEAD_DIM
SB_PAIRS = D_MODEL // SB_PAIR
Q_BLOCK = 128
MASKED_LOGIT = -1e30
EXIT_LOGIT = -110.0
CHUNK = 128
SG_GROUPS = 8
SG_HEAD_DIM = D_MODEL // SG_GROUPS
SSM_GROUP = 16
SSM_GROUPS = 64
SSM_STATE = 64
SSM_SLABS = 8
SSM_SLAB_GROUPS = SSM_GROUPS // SSM_SLABS
SSM_SLAB_STATES = SSM_SLAB_GROUPS * SSM_STATE
D_FF = 2816
CONV_K = 3
DEPTH = 4
N_MIXERS = 3

V7X_VMEM_LIMIT_BYTES = 56 * 1024 * 1024
V7X_FFN_VMEM_LIMIT_BYTES = 60 * 1024 * 1024
SUBLANES = 8
LANES = 128

FFN_ROWS = 1024
FFN_FC = 256
QKV_ROWS = 1024
SB_QBLOCKS_PER_STEP = 4
SB_BAND = 64
SG_ROWS = 512
SSM_STEPS = 128


def _rms(x, g):
    return x * lax.rsqrt(jnp.mean(x * x, axis=-1, keepdims=True) + EPS) * g


def _sigmoid(x):
    return 1.0 / (1.0 + jnp.exp(-x))


def _gelu(x):
    c = math.sqrt(2.0 / math.pi)
    return 0.5 * x * (1.0 + jnp.tanh(c * (x + 0.044715 * (x * x * x))))


def _resident(shape):
    nd = len(shape)
    return pl.BlockSpec(shape, lambda *_: (0,) * nd, pipeline_mode=pl.Buffered(1))


def _params(n_axes, vmem_limit_bytes=V7X_VMEM_LIMIT_BYTES):
    return pltpu.CompilerParams(
        dimension_semantics=("arbitrary",) * n_axes,
        vmem_limit_bytes=vmem_limit_bytes,
    )


def _ffn_kernel(*refs, glu, final, tl, fc, n_chunks):
    if final:
        (x_ref, m_ref, wm_ref, g_ref, wa_ref, wg_ref, cwa_ref, cwg_ref, cba_ref, cbg_ref, wd_ref,
         fg_ref, o_ref, xn_s, hsa, hsg, cara, carg, act) = refs
    else:
        (x_ref, m_ref, wm_ref, g_ref, wa_ref, wg_ref, cwa_ref, cwg_ref, cba_ref, cbg_ref, wd_ref,
         o_ref, xn_s, hsa, hsg, cara, carg, act) = refs
        fg_ref = None

    @pl.when(pl.program_id(1) == 0)
    def _():
        cara[...] = jnp.zeros_like(cara)
        carg[...] = jnp.zeros_like(carg)

    mm = jnp.dot(m_ref[...], wm_ref[...], preferred_element_type=F32)
    if glu:
        mix = mm[:, :D_MODEL] * _sigmoid(mm[:, D_MODEL:])
    else:
        mix = mm
    x1 = x_ref[...] + mix
    o_ref[...] = x1
    xn_s[...] = _rms(x1, g_ref[...]).astype(BF16)

    def conv(hs, car, cw_ref, cb_ref, h, cols):
        hs[0:SUBLANES, :] = car[:, cols]
        hs[SUBLANES:SUBLANES + tl, :] = h
        car[:, cols] = h[tl - SUBLANES:tl, :]
        w = cw_ref[:, cols]
        y = (w[2:3, :] * hs[SUBLANES:SUBLANES + tl, :]
             + w[1:2, :] * hs[SUBLANES - 1:SUBLANES - 1 + tl, :]
             + w[0:1, :] * hs[SUBLANES - 2:SUBLANES - 2 + tl, :])
        return y + cb_ref[:, cols]

    for c in range(n_chunks):
        cols = slice(c * fc, (c + 1) * fc)
        xn = xn_s[...]
        ha = jnp.dot(xn, wa_ref[:, cols], preferred_element_type=F32)
        hg = jnp.dot(xn, wg_ref[:, cols], preferred_element_type=F32)
        ya = conv(hsa, cara, cwa_ref, cba_ref, ha, cols)
        yg = conv(hsg, carg, cwg_ref, cbg_ref, hg, cols)
        act[:, cols] = (yg * _sigmoid(yg) * ya).astype(BF16)

    head = (n_chunks - 1) * fc
    out = (o_ref[...] + jnp.dot(act[:, :head], wd_ref[:head, :], preferred_element_type=F32)
           + jnp.dot(act[:, head:], wd_ref[head:, :], preferred_element_type=F32))
    if final:
        out = _rms(out, fg_ref[...])
    o_ref[...] = out


def _ffn_call(x, m, wm, g, wa, wg, cwa, cwg, cba, cbg, wd, fg, *, glu, tl=FFN_ROWS, fc=FFN_FC):
    B, L, D = x.shape
    F = wa.shape[1]
    dm = m.shape[2]
    final = fg is not None
    n_chunks = F // fc
    row = lambda b, i: (b, i, 0)
    in_specs = [
        pl.BlockSpec((None, tl, D), row),
        pl.BlockSpec((None, tl, dm), row),
        _resident(wm.shape), _resident(g.shape), _resident(wa.shape), _resident(wg.shape),
        _resident(cwa.shape), _resident(cwg.shape), _resident(cba.shape), _resident(cbg.shape),
        _resident(wd.shape),
    ]
    args = [x, m, wm, g, wa, wg, cwa, cwg, cba, cbg, wd]
    if final:
        in_specs.append(_resident(fg.shape))
        args.append(fg)
    return pl.pallas_call(
        functools.partial(_ffn_kernel, glu=glu, final=final, tl=tl, fc=fc, n_chunks=n_chunks),
        grid=(B, L // tl),
        in_specs=in_specs,
        out_specs=pl.BlockSpec((None, tl, D), row),
        out_shape=jax.ShapeDtypeStruct((B, L, D), F32),
        scratch_shapes=[
            pltpu.VMEM((tl, D), BF16),
            pltpu.VMEM((tl + SUBLANES, fc), F32),
            pltpu.VMEM((tl + SUBLANES, fc), F32),
            pltpu.VMEM((SUBLANES, F), F32),
            pltpu.VMEM((SUBLANES, F), F32),
            pltpu.VMEM((tl, F), BF16),
        ],
        compiler_params=_params(2, V7X_FFN_VMEM_LIMIT_BYTES),
        name="ffn",
    )(*args)


def _qkv_kernel(x_ref, g_ref, wq_ref, wkt_ref, wv_ref, q_ref, kt_ref, v_ref, *, tl):
    xn = _rms(x_ref[...], g_ref[...]).astype(BF16)
    q = jnp.dot(xn, wq_ref[...], preferred_element_type=F32) * (SB_HEAD_DIM ** -0.5)
    q_ref[...] = q.astype(BF16)
    v_ref[...] = jnp.dot(xn, wv_ref[...], preferred_element_type=F32).astype(BF16)
    kt = lax.dot_general(wkt_ref[...], xn, (((1,), (1,)), ((), ())), preferred_element_type=F32)
    kt = kt.astype(BF16)
    for t in range(tl // Q_BLOCK):
        kt_ref[t] = kt[:, t * Q_BLOCK:(t + 1) * Q_BLOCK]


def _qkv_call(x, g, wq, wkt, wv, *, tl=QKV_ROWS):
    B, L, D = x.shape
    nk = tl // Q_BLOCK
    row = lambda b, i: (b, i, 0)
    return pl.pallas_call(
        functools.partial(_qkv_kernel, tl=tl),
        grid=(B, L // tl),
        in_specs=[pl.BlockSpec((None, tl, D), row), _resident(g.shape), _resident(wq.shape),
                  _resident(wkt.shape), _resident(wv.shape)],
        out_specs=[pl.BlockSpec((None, tl, D), row),
                   pl.BlockSpec((None, nk, D, Q_BLOCK), lambda b, i: (b, i, 0, 0)),
                   pl.BlockSpec((None, tl, D), row)],
        out_shape=[jax.ShapeDtypeStruct((B, L, D), BF16),
                   jax.ShapeDtypeStruct((B, L // Q_BLOCK, D, Q_BLOCK), BF16),
                   jax.ShapeDtypeStruct((B, L, D), BF16)],
        compiler_params=_params(2),
        name="sb_qkv",
    )(x, g, wq, wkt, wv)


def _sb_kernel(q_ref, kt_ref, v_ref, o_ref, acc, csum, kbd, vbd, *, nk, qb):
    step = pl.program_id(1)
    W = 2 * Q_BLOCK
    pairs = range(SB_PAIRS)
    col = lambda hp: slice(hp * SB_PAIR, (hp + 1) * SB_PAIR)

    @pl.when(step == 0)
    def _():
        d_row = lax.broadcasted_iota(jnp.int32, (SB_PAIR, Q_BLOCK), 0)
        d_col = lax.broadcasted_iota(jnp.int32, (Q_BLOCK, SB_PAIR), 1)

        def build(jb, carry):
            k0 = pl.multiple_of(jb * Q_BLOCK, Q_BLOCK)
            for hp in pairs:
                kt = kt_ref[jb, col(hp), :]
                zk = jnp.zeros_like(kt)
                kbd[jb, hp] = jnp.concatenate([jnp.where(d_row < SB_HEAD_DIM, kt, zk),
                                               jnp.where(d_row >= SB_HEAD_DIM, kt, zk)], axis=1)
                v2 = v_ref[pl.ds(k0, Q_BLOCK), col(hp)]
                zv = jnp.zeros_like(v2)
                vbd[jb, hp] = jnp.concatenate([jnp.where(d_col < SB_HEAD_DIM, v2, zv),
                                               jnp.where(d_col >= SB_HEAD_DIM, v2, zv)], axis=0)
            return carry

        lax.fori_loop(0, nk, build, 0)

    r = lax.broadcasted_iota(jnp.int32, (W, W), 0)
    c = lax.broadcasted_iota(jnp.int32, (W, W), 1)
    same = (r < Q_BLOCK) == (c < Q_BLOCK)
    tri = jnp.where(same & (r > c), 1.0, 0.0).astype(BF16)
    ones = jnp.where(same, 1.0, 0.0).astype(BF16)
    tri_ones = jnp.concatenate([tri, ones], axis=1)

    t_row = lax.broadcasted_iota(jnp.int32, (Q_BLOCK, W), 0)
    s_col = lax.broadcasted_iota(jnp.int32, (Q_BLOCK, W), 1)
    s_col = jnp.where(s_col >= Q_BLOCK, s_col - Q_BLOCK, s_col)
    causal = s_col < t_row

    def query_block(qq, carry):
        i = step * qb + qq
        row0 = pl.multiple_of(qq * Q_BLOCK, Q_BLOCK)
        rows = pl.ds(row0, Q_BLOCK)

        def scores(j, hp, masked, off, nr):
            z = jnp.dot(q_ref[pl.ds(row0 + off, nr), col(hp)], kbd[j, hp], preferred_element_type=F32)
            ls = jnp.minimum(z, 0.0) - jnp.log(1.0 + jnp.exp2(jnp.abs(z) * -LOG2E))
            l1 = ls - z
            if masked:
                l1 = jnp.where(causal, l1, 0.0)
                ls = jnp.where(causal, ls, MASKED_LOGIT)
            return ls, l1.astype(BF16)

        def weights(hp, ls, l1, first, off, nr):
            es = jnp.dot(l1, tri_ones, preferred_element_type=F32)
            if first:
                w = jnp.exp(ls + es[:, :W])
                later = es[:, W:]
            else:
                later = csum[hp, off:off + nr]
                w = jnp.exp(ls + es[:, :W] + later)
                later = later + es[:, W:]
            csum[hp, off:off + nr] = later
            return w.astype(BF16), later

        def tiles(js, spans, first_masked):
            staged, wts, tops = {}, {}, {}

            def stage_scores(k):
                staged[k] = [scores(js[k], hp, first_masked and k == 0, *spans[k]) for hp in pairs]

            def stage_weights(k):
                wts[k] = []
                for hp in pairs:
                    w, later = weights(hp, *staged[k][hp], first_masked and k == 0, *spans[k])
                    tops[k] = later if hp == 0 else jnp.maximum(tops[k], later)
                    wts[k].append(w)

            def stage_values(k):
                off, nr = spans[k]
                for hp in pairs:
                    pv = jnp.dot(wts[k][hp], vbd[js[k], hp], preferred_element_type=F32)
                    if first_masked and k == 0:
                        acc[hp, off:off + nr] = pv
                    else:
                        acc[hp, off:off + nr] += pv

            for k in range(len(js) + 2):
                if k < len(js):
                    stage_scores(k)
                if 0 <= k - 1 < len(js):
                    stage_weights(k - 1)
                if 0 <= k - 2 < len(js):
                    stage_values(k - 2)
            return [tops[k] for k in range(len(js))]

        live = lambda t: (jnp.max(t) >= EXIT_LOGIT).astype(jnp.int32)
        ALL, BAND, REST = (0, Q_BLOCK), (0, SB_BAND), (SB_BAND, Q_BLOCK - SB_BAND)

        def head_blocks():
            t0, t1, t2 = tiles([i, i - 1, i - 2], [ALL, ALL, BAND], True)
            rest = lax.cond(live(t1[SB_BAND:]) > 0,
                            lambda: live(tiles([i - 2], [REST], False)[0]),
                            lambda: jnp.int32(0))
            return i - 3, live(t2), rest

        def first_block():
            (t0,) = tiles([i], [ALL], True)
            return i - 1, live(t0[:SB_BAND]), live(t0[SB_BAND:])

        def cond(c):
            j, band, rest = c
            return jnp.logical_and(j >= 0, (band + rest) > 0)

        def full_block(j):
            (t,) = tiles([j], [ALL], False)
            return live(t[:SB_BAND]), live(t[SB_BAND:])

        def band_block(j):
            (t,) = tiles([j], [BAND], False)
            return live(t), jnp.int32(0)

        def body(c):
            j, _, rest = c
            band, rest = lax.cond(rest > 0, lambda: full_block(j), lambda: band_block(j))
            return j - 1, band, rest

        lax.while_loop(cond, body, lax.cond(i >= 2, head_blocks, first_block))

        for hp in pairs:
            o_ref[rows, col(hp)] = acc[hp].astype(BF16)
        return carry

    lax.fori_loop(0, qb, query_block, 0)


def _sb_call(q, kt, v, *, qb=SB_QBLOCKS_PER_STEP):
    B, L, D = q.shape
    nq = L // Q_BLOCK
    rows = qb * Q_BLOCK
    return pl.pallas_call(
        functools.partial(_sb_kernel, nk=nq, qb=qb),
        grid=(B, nq // qb),
        in_specs=[pl.BlockSpec((None, rows, D), lambda b, i: (b, i, 0)),
                  pl.BlockSpec((None, nq, D, Q_BLOCK), lambda b, i: (b, 0, 0, 0)),
                  pl.BlockSpec((None, L, D), lambda b, i: (b, 0, 0))],
        out_specs=pl.BlockSpec((None, rows, D), lambda b, i: (b, i, 0)),
        out_shape=jax.ShapeDtypeStruct((B, L, D), BF16),
        scratch_shapes=[pltpu.VMEM((SB_PAIRS, Q_BLOCK, SB_PAIR), F32),
                        pltpu.VMEM((SB_PAIRS, Q_BLOCK, 2 * Q_BLOCK), F32),
                        pltpu.VMEM((nq, SB_PAIRS, SB_PAIR, 2 * Q_BLOCK), BF16),
                        pltpu.VMEM((nq, SB_PAIRS, 2 * Q_BLOCK, SB_PAIR), BF16)],
        compiler_params=_params(2),
        name="sb_attn",
    )(q, kt, v)


def _sg_kernel(x_ref, g_ref, win_ref, vg_ref, ws_ref, bst_ref, m_ref, u_s, v_s, *, tl):
    xn = _rms(x_ref[...], g_ref[...]).astype(BF16)
    h = _gelu(jnp.dot(xn, win_ref[...], preferred_element_type=F32))
    u_s[...] = h[:, :D_MODEL]
    v_s[...] = _rms(h[:, D_MODEL:], vg_ref[...]).astype(BF16)
    r = lax.broadcasted_iota(jnp.int32, (CHUNK, CHUNK), 0)
    c = lax.broadcasted_iota(jnp.int32, (CHUNK, CHUNK), 1)
    for g in range(SG_GROUPS):
        cols = slice(g * SG_HEAD_DIM, (g + 1) * SG_HEAD_DIM)
        w = jnp.where(r >= c, ws_ref[g], 0.0).astype(BF16)
        bias = jnp.broadcast_to(bst_ref[:, g:g + 1], (CHUNK, SG_HEAD_DIM))
        for ch in range(tl // CHUNK):
            rows = slice(ch * CHUNK, (ch + 1) * CHUNK)
            sv = jnp.dot(w, v_s[rows, cols], preferred_element_type=F32) + bias
            m_ref[rows, cols] = (u_s[rows, cols] * sv).astype(BF16)


def _sg_call(x, g, win, vg, ws, bst, *, tl=SG_ROWS):
    B, L, D = x.shape
    row = lambda b, i: (b, i, 0)
    return pl.pallas_call(
        functools.partial(_sg_kernel, tl=tl),
        grid=(B, L // tl),
        in_specs=[pl.BlockSpec((None, tl, D), row), _resident(g.shape), _resident(win.shape),
                  _resident(vg.shape), _resident(ws.shape), _resident(bst.shape)],
        out_specs=pl.BlockSpec((None, tl, D), row),
        out_shape=jax.ShapeDtypeStruct((B, L, D), BF16),
        scratch_shapes=[pltpu.VMEM((tl, D), F32), pltpu.VMEM((tl, D), BF16)],
        compiler_params=_params(2),
        name="sg_gate",
    )(x, g, win, vg, ws, bst)


def _ssm_prep_kernel(lre_ref, lim_ref, ldt_ref, cre_ref, cim_ref, ar_ref, ai_ref, ctr_ref, cti_ref):
    lr = jnp.minimum(lre_ref[...], -1e-4)
    li = lim_ref[...]
    dt = jnp.exp(ldt_ref[...])
    mag = jnp.exp(dt * lr)
    ar = mag * jnp.cos(dt * li)
    ai = mag * jnp.sin(dt * li)
    den = lr * lr + li * li
    cr = ((ar - 1.0) * lr + ai * li) / den
    ci = (ai * lr - (ar - 1.0) * li) / den
    ar_ref[...] = ar
    ai_ref[...] = ai
    cre = cre_ref[...]
    cim = cim_ref[...]
    cr3 = cr[:, None, :]
    ci3 = ci[:, None, :]
    ctr_ref[...] = cre * cr3 - cim * ci3
    cti_ref[...] = cre * ci3 + cim * cr3


def _ssm_prep_call(lam_re, lam_im, log_dt, c_re, c_im):
    G, P = lam_re.shape
    H = c_re.shape[1]
    return pl.pallas_call(
        _ssm_prep_kernel,
        out_shape=[jax.ShapeDtypeStruct((G, P), F32), jax.ShapeDtypeStruct((G, P), F32),
                   jax.ShapeDtypeStruct((G, H, P), F32), jax.ShapeDtypeStruct((G, H, P), F32)],
        name="ssm_prep",
    )(lam_re, lam_im, log_dt.reshape(G, 1), c_re, c_im)


def _ssm_kernel(x_ref, g_ref, win_ref, wb_ref, wc_ref, ar_ref, ai_ref, d_ref, o_ref,
                u_lb, bu, y_lb, st, *, tl, nb):
    S = SSM_SLAB_STATES

    @pl.when(pl.program_id(0) == 0)
    def _():
        st[...] = jnp.zeros_like(st)

    for b in range(nb):
        xn = _rms(x_ref[b], g_ref[...]).astype(BF16)
        u = jnp.dot(xn, win_ref[...], preferred_element_type=F32)
        for j in range(SSM_SLABS):
            u_lb[j, pl.ds(b, tl, stride=nb), :] = u[:, j * LANES:(j + 1) * LANES]

    for j in range(SSM_SLABS):
        uj = u_lb[j]
        bu[...] = jnp.dot(uj.astype(BF16), wb_ref[j], preferred_element_type=F32)
        ar = jnp.broadcast_to(ar_ref[j:j + 1, :], (nb, S))
        ai = jnp.broadcast_to(ai_ref[j:j + 1, :], (nb, S))

        def step(t, carry):
            s_re, s_im = carry
            r0 = pl.multiple_of(t * nb, nb)
            n_re = ar * s_re - ai * s_im + bu[pl.ds(r0, nb), 0:S]
            n_im = ar * s_im + ai * s_re + bu[pl.ds(r0, nb), S:2 * S]
            bu[pl.ds(r0, nb), 0:S] = n_re
            bu[pl.ds(r0, nb), S:2 * S] = n_im
            return n_re, n_im

        s_re, s_im = lax.fori_loop(0, tl, step, (st[j, :, 0:S], st[j, :, S:2 * S]), unroll=True)
        st[j, :, 0:S] = s_re
        st[j, :, S:2 * S] = s_im
        y = jnp.dot(bu[...].astype(BF16), wc_ref[j], preferred_element_type=F32)
        y_lb[j] = _gelu(y + d_ref[j:j + 1, :] * uj)

    for b in range(nb):
        for j in range(SSM_SLABS):
            o_ref[b, :, j * LANES:(j + 1) * LANES] = y_lb[j, pl.ds(b, tl, stride=nb), :].astype(BF16)


def _ssm_call(x, g, win, wb, wc, ar, ai, d, *, tl=SSM_STEPS):
    B, L, D = x.shape
    R = B * tl
    blk = lambda i: (0, i, 0)
    return pl.pallas_call(
        functools.partial(_ssm_kernel, tl=tl, nb=B),
        grid=(L // tl,),
        in_specs=[pl.BlockSpec((B, tl, D), blk), _resident(g.shape), _resident(win.shape),
                  _resident(wb.shape), _resident(wc.shape), _resident(ar.shape), _resident(ai.shape),
                  _resident(d.shape)],
        out_specs=pl.BlockSpec((B, tl, D), blk),
        out_shape=jax.ShapeDtypeStruct((B, L, D), BF16),
        scratch_shapes=[pltpu.VMEM((SSM_SLABS, R, LANES), F32),
                        pltpu.VMEM((R, 2 * SSM_SLAB_STATES), F32),
                        pltpu.VMEM((SSM_SLABS, R, LANES), F32),
                        pltpu.VMEM((SSM_SLABS, B, 2 * SSM_SLAB_STATES), F32)],
        compiler_params=_params(1),
        name="ssm_scan",
    )(x, g, win, wb, wc, ar, ai, d)


def _ssm_block_weights(b_re, b_im, ct_re, ct_im):
    J, GL, P, H = SSM_SLABS, SSM_SLAB_GROUPS, SSM_STATE, SSM_GROUP
    eye = jnp.eye(GL, dtype=F32)

    def in_side(b):
        bt = b.reshape(J, GL, P, H).transpose(0, 1, 3, 2)
        return (bt[:, :, :, None, :] * eye[None, :, None, :, None]).reshape(J, GL * H, GL * P)

    def out_side(c):
        ct = c.reshape(J, GL, H, P).transpose(0, 1, 3, 2)
        return (ct[:, :, :, None, :] * eye[None, :, None, :, None]).reshape(J, GL * P, GL * H)

    wb = jnp.concatenate([in_side(b_re), in_side(b_im)], axis=2).astype(BF16)
    wc = jnp.concatenate([out_side(ct_re), -out_side(ct_im)], axis=1).astype(BF16)
    return wb, wc


def kernel(x, norm_g, final_norm_g, sb_w_qkv, sb_w_o, sg_w_in, sg_norm_g, sg_w_s, sg_b, sg_w_o,
           ssm_w_in, ssm_lam_re, ssm_lam_im, ssm_log_dt, ssm_b_re, ssm_b_im, ssm_c_re, ssm_c_im,
           ssm_d, ssm_w_glu, ffn_w_up, ffn_conv_w, ffn_conv_b, ffn_w_down):
    D = D_MODEL
    F = D_FF
    for i in range(DEPTH):
        mixer = i % N_MIXERS
        j = i // N_MIXERS
        g1 = norm_g[i, 0].reshape(1, D)
        g2 = norm_g[i, 1].reshape(1, D)
        if mixer == 0:
            w = sb_w_qkv[j]
            q, kt, v = _qkv_call(x, g1, w[:, :D].astype(BF16), w[:, D:2 * D].T.astype(BF16),
                                 w[:, 2 * D:].astype(BF16))
            m = _sb_call(q, kt, v)
            wm = sb_w_o[j].astype(BF16)
            glu = False
        elif mixer == 1:
            m = _sg_call(x, g1, sg_w_in[j].astype(BF16), sg_norm_g[j].reshape(1, D), sg_w_s[j],
                         sg_b[j].T)
            wm = sg_w_o[j].astype(BF16)
            glu = False
        else:
            ar, ai, ct_re, ct_im = _ssm_prep_call(ssm_lam_re[j], ssm_lam_im[j], ssm_log_dt[j],
                                                  ssm_c_re[j], ssm_c_im[j])
            wb, wc = _ssm_block_weights(ssm_b_re[j], ssm_b_im[j], ct_re, ct_im)
            m = _ssm_call(x, g1, ssm_w_in[j].astype(BF16), wb, wc,
                          ar.reshape(SSM_SLABS, SSM_SLAB_STATES), ai.reshape(SSM_SLABS, SSM_SLAB_STATES),
                          ssm_d[j].reshape(SSM_SLABS, LANES))
            wm = ssm_w_glu[j].astype(BF16)
            glu = True
        wu = ffn_w_up[i]
        cw = ffn_conv_w[i]
        cb = ffn_conv_b[i].reshape(1, 2 * F)
        fg = final_norm_g.reshape(1, D) if i == DEPTH - 1 else None
        x = _ffn_call(x, m, wm, g2, wu[:, :F].astype(BF16), wu[:, F:].astype(BF16),
                      cw[:, :F], cw[:, F:], cb[:, :F], cb[:, F:], ffn_w_down[i].astype(BF16), fg,
                      glu=glu)
    return x
```

```python
import functools
import math

import jax
import jax.numpy as jnp
from jax import lax
from jax.experimental import pallas as pl
from jax.experimental.pallas import tpu as pltpu

F32 = jnp.float32
BF16 = jnp.bfloat16

D_MODEL = 1024
EPS = 1e-6
LOG2E = math.log2(math.e)
SB_HEAD_DIM = 64
SB_PAIR = 2 * SB_HEAD_DIM
SB_PAIRS = D_MODEL // SB_PAIR
Q_BLOCK = 128
MASKED_LOGIT = -1e30
EXIT_LOGIT = -110.0
CHUNK = 128
SG_GROUPS = 8
SG_HEAD_DIM = D_MODEL // SG_GROUPS
SSM_GROUP = 16
SSM_GROUPS = 64
SSM_STATE = 64
SSM_SLABS = 8
SSM_SLAB_GROUPS = SSM_GROUPS // SSM_SLABS
SSM_SLAB_STATES = SSM_SLAB_GROUPS * SSM_STATE
D_FF = 2816
CONV_K = 3
DEPTH = 4
N_MIXERS = 3

V7X_VMEM_LIMIT_BYTES = 56 * 1024 * 1024
V7X_FFN_VMEM_LIMIT_BYTES = 60 * 1024 * 1024
SUBLANES = 8
LANES = 128

FFN_ROWS = 1024
FFN_FC = 256
QKV_ROWS = 1024
SB_QBLOCKS_PER_STEP = 4
SB_FUSED_BLOCKS = 3
SG_ROWS = 512
SSM_STEPS = 128


def _rms(x, g):
    return x * lax.rsqrt(jnp.mean(x * x, axis=-1, keepdims=True) + EPS) * g


def _sigmoid(x):
    return 1.0 / (1.0 + jnp.exp(-x))


def _gelu(x):
    c = math.sqrt(2.0 / math.pi)
    return 0.5 * x * (1.0 + jnp.tanh(c * (x + 0.044715 * (x * x * x))))


def _resident(shape):
    nd = len(shape)
    return pl.BlockSpec(shape, lambda *_: (0,) * nd, pipeline_mode=pl.Buffered(1))


def _params(n_axes, vmem_limit_bytes=V7X_VMEM_LIMIT_BYTES):
    return pltpu.CompilerParams(
        dimension_semantics=("arbitrary",) * n_axes,
        vmem_limit_bytes=vmem_limit_bytes,
    )


def _ffn_kernel(*refs, glu, final, tl, fc, n_chunks):
    if final:
        (x_ref, m_ref, wm_ref, g_ref, wa_ref, wg_ref, cwa_ref, cwg_ref, cba_ref, cbg_ref, wd_ref,
         fg_ref, o_ref, xn_s, hsa, hsg, cara, carg, act) = refs
    else:
        (x_ref, m_ref, wm_ref, g_ref, wa_ref, wg_ref, cwa_ref, cwg_ref, cba_ref, cbg_ref, wd_ref,
         o_ref, xn_s, hsa, hsg, cara, carg, act) = refs
        fg_ref = None

    @pl.when(pl.program_id(1) == 0)
    def _():
        cara[...] = jnp.zeros_like(cara)
        carg[...] = jnp.zeros_like(carg)

    mm = jnp.dot(m_ref[...], wm_ref[...], preferred_element_type=F32)
    if glu:
        mix = mm[:, :D_MODEL] * _sigmoid(mm[:, D_MODEL:])
    else:
        mix = mm
    x1 = x_ref[...] + mix
    o_ref[...] = x1
    xn_s[...] = _rms(x1, g_ref[...]).astype(BF16)

    def conv(hs, car, cw_ref, cb_ref, h, cols):
        hs[0:SUBLANES, :] = car[:, cols]
        hs[SUBLANES:SUBLANES + tl, :] = h
        car[:, cols] = h[tl - SUBLANES:tl, :]
        w = cw_ref[:, cols]
        y = (w[2:3, :] * hs[SUBLANES:SUBLANES + tl, :]
             + w[1:2, :] * hs[SUBLANES - 1:SUBLANES - 1 + tl, :]
             + w[0:1, :] * hs[SUBLANES - 2:SUBLANES - 2 + tl, :])
        return y + cb_ref[:, cols]

    for c in range(n_chunks):
        cols = slice(c * fc, (c + 1) * fc)
        xn = xn_s[...]
        ha = jnp.dot(xn, wa_ref[:, cols], preferred_element_type=F32)
        hg = jnp.dot(xn, wg_ref[:, cols], preferred_element_type=F32)
        ya = conv(hsa, cara, cwa_ref, cba_ref, ha, cols)
        yg = conv(hsg, carg, cwg_ref, cbg_ref, hg, cols)
        act[:, cols] = (yg * _sigmoid(yg) * ya).astype(BF16)

    head = (n_chunks - 1) * fc
    out = (o_ref[...] + jnp.dot(act[:, :head], wd_ref[:head, :], preferred_element_type=F32)
           + jnp.dot(act[:, head:], wd_ref[head:, :], preferred_element_type=F32))
    if final:
        out = _rms(out, fg_ref[...])
    o_ref[...] = out


def _ffn_call(x, m, wm, g, wa, wg, cwa, cwg, cba, cbg, wd, fg, *, glu, tl=FFN_ROWS, fc=FFN_FC):
    B, L, D = x.shape
    F = wa.shape[1]
    dm = m.shape[2]
    final = fg is not None
    n_chunks = F // fc
    row = lambda b, i: (b, i, 0)
    in_specs = [
        pl.BlockSpec((None, tl, D), row),
        pl.BlockSpec((None, tl, dm), row),
        _resident(wm.shape), _resident(g.shape), _resident(wa.shape), _resident(wg.shape),
        _resident(cwa.shape), _resident(cwg.shape), _resident(cba.shape), _resident(cbg.shape),
        _resident(wd.shape),
    ]
    args = [x, m, wm, g, wa, wg, cwa, cwg, cba, cbg, wd]
    if final:
        in_specs.append(_resident(fg.shape))
        args.append(fg)
    return pl.pallas_call(
        functools.partial(_ffn_kernel, glu=glu, final=final, tl=tl, fc=fc, n_chunks=n_chunks),
        grid=(B, L // tl),
        in_specs=in_specs,
        out_specs=pl.BlockSpec((None, tl, D), row),
        out_shape=jax.ShapeDtypeStruct((B, L, D), F32),
        scratch_shapes=[
            pltpu.VMEM((tl, D), BF16),
            pltpu.VMEM((tl + SUBLANES, fc), F32),
            pltpu.VMEM((tl + SUBLANES, fc), F32),
            pltpu.VMEM((SUBLANES, F), F32),
            pltpu.VMEM((SUBLANES, F), F32),
            pltpu.VMEM((tl, F), BF16),
        ],
        compiler_params=_params(2, V7X_FFN_VMEM_LIMIT_BYTES),
        name="ffn",
    )(*args)


def _qkv_kernel(x_ref, g_ref, wq_ref, wkt_ref, wv_ref, q_ref, kt_ref, v_ref, *, tl):
    xn = _rms(x_ref[...], g_ref[...]).astype(BF16)
    q = jnp.dot(xn, wq_ref[...], preferred_element_type=F32) * (SB_HEAD_DIM ** -0.5)
    q_ref[...] = q.astype(BF16)
    v_ref[...] = jnp.dot(xn, wv_ref[...], preferred_element_type=F32).astype(BF16)
    kt = lax.dot_general(wkt_ref[...], xn, (((1,), (1,)), ((), ())), preferred_element_type=F32)
    kt = kt.astype(BF16)
    for t in range(tl // Q_BLOCK):
        kt_ref[t] = kt[:, t * Q_BLOCK:(t + 1) * Q_BLOCK]


def _qkv_call(x, g, wq, wkt, wv, *, tl=QKV_ROWS):
    B, L, D = x.shape
    nk = tl // Q_BLOCK
    row = lambda b, i: (b, i, 0)
    return pl.pallas_call(
        functools.partial(_qkv_kernel, tl=tl),
        grid=(B, L // tl),
        in_specs=[pl.BlockSpec((None, tl, D), row), _resident(g.shape), _resident(wq.shape),
                  _resident(wkt.shape), _resident(wv.shape)],
        out_specs=[pl.BlockSpec((None, tl, D), row),
                   pl.BlockSpec((None, nk, D, Q_BLOCK), lambda b, i: (b, i, 0, 0)),
                   pl.BlockSpec((None, tl, D), row)],
        out_shape=[jax.ShapeDtypeStruct((B, L, D), BF16),
                   jax.ShapeDtypeStruct((B, L // Q_BLOCK, D, Q_BLOCK), BF16),
                   jax.ShapeDtypeStruct((B, L, D), BF16)],
        compiler_params=_params(2),
        name="sb_qkv",
    )(x, g, wq, wkt, wv)


def _sb_kernel(q_ref, kt_ref, v_ref, o_ref, acc, csum, kbd, vbd, *, nk, qb):
    step = pl.program_id(1)
    W = 2 * Q_BLOCK
    pairs = range(SB_PAIRS)
    col = lambda hp: slice(hp * SB_PAIR, (hp + 1) * SB_PAIR)

    @pl.when(step == 0)
    def _():
        d_row = lax.broadcasted_iota(jnp.int32, (SB_PAIR, Q_BLOCK), 0)
        d_col = lax.broadcasted_iota(jnp.int32, (Q_BLOCK, SB_PAIR), 1)

        def build(jb, carry):
            k0 = pl.multiple_of(jb * Q_BLOCK, Q_BLOCK)
            for hp in pairs:
                kt = kt_ref[jb, col(hp), :]
                zk = jnp.zeros_like(kt)
                kbd[jb, hp] = jnp.concatenate([jnp.where(d_row < SB_HEAD_DIM, kt, zk),
                                               jnp.where(d_row >= SB_HEAD_DIM, kt, zk)], axis=1)
                v2 = v_ref[pl.ds(k0, Q_BLOCK), col(hp)]
                zv = jnp.zeros_like(v2)
                vbd[jb, hp] = jnp.concatenate([jnp.where(d_col < SB_HEAD_DIM, v2, zv),
                                               jnp.where(d_col >= SB_HEAD_DIM, v2, zv)], axis=0)
            return carry

        lax.fori_loop(0, nk, build, 0)

    r = lax.broadcasted_iota(jnp.int32, (W, W), 0)
    c = lax.broadcasted_iota(jnp.int32, (W, W), 1)
    same = (r < Q_BLOCK) == (c < Q_BLOCK)
    tri = jnp.where(same & (r > c), 1.0, 0.0).astype(BF16)
    ones = jnp.where(same, 1.0, 0.0).astype(BF16)
    tri_ones = jnp.concatenate([tri, ones], axis=1)

    t_row = lax.broadcasted_iota(jnp.int32, (Q_BLOCK, W), 0)
    s_col = lax.broadcasted_iota(jnp.int32, (Q_BLOCK, W), 1)
    s_col = jnp.where(s_col >= Q_BLOCK, s_col - Q_BLOCK, s_col)
    causal = s_col < t_row

    def query_block(qq, carry):
        i = step * qb + qq
        rows = pl.ds(pl.multiple_of(qq * Q_BLOCK, Q_BLOCK), Q_BLOCK)

        def scores(j, hp, masked):
            z = jnp.dot(q_ref[rows, col(hp)], kbd[j, hp], preferred_element_type=F32)
            ls = jnp.minimum(z, 0.0) - jnp.log(1.0 + jnp.exp2(jnp.abs(z) * -LOG2E))
            l1 = ls - z
            if masked:
                l1 = jnp.where(causal, l1, 0.0)
                ls = jnp.where(causal, ls, MASKED_LOGIT)
            return ls, l1.astype(BF16)

        def weights(hp, ls, l1, first):
            es = jnp.dot(l1, tri_ones, preferred_element_type=F32)
            if first:
                w = jnp.exp(ls + es[:, :W])
                later = es[:, W:]
            else:
                later = csum[hp]
                w = jnp.exp(ls + es[:, :W] + later)
                later = later + es[:, W:]
            csum[hp] = later
            return w.astype(BF16), later

        def tiles(js, first_masked):
            staged, wts = {}, {}
            top = [None]

            def stage_scores(k):
                staged[k] = [scores(js[k], hp, first_masked and k == 0) for hp in pairs]

            def stage_weights(k):
                top[0] = None
                wts[k] = []
                for hp in pairs:
                    w, later = weights(hp, *staged[k][hp], first_masked and k == 0)
                    top[0] = later if top[0] is None else jnp.maximum(top[0], later)
                    wts[k].append(w)

            def stage_values(k):
                for hp in pairs:
                    pv = jnp.dot(wts[k][hp], vbd[js[k], hp], preferred_element_type=F32)
                    if first_masked and k == 0:
                        acc[hp] = pv
                    else:
                        acc[hp] += pv

            for k in range(len(js) + 2):
                if k < len(js):
                    stage_scores(k)
                if 0 <= k - 1 < len(js):
                    stage_weights(k - 1)
                if 0 <= k - 2 < len(js):
                    stage_values(k - 2)
            return (jnp.max(top[0]) >= EXIT_LOGIT).astype(jnp.int32)

        def cond(c):
            j, live = c
            return jnp.logical_and(j >= 0, live > 0)

        def body(c):
            j, _ = c
            return j - 1, tiles([j], False)

        n0 = SB_FUSED_BLOCKS
        start = lax.cond(i >= n0 - 1,
                         lambda: (i - n0, tiles([i - k for k in range(n0)], True)),
                         lambda: (i - 1, tiles([i], True)))
        lax.while_loop(cond, body, start)

        for hp in pairs:
            o_ref[rows, col(hp)] = acc[hp].astype(BF16)
        return carry

    lax.fori_loop(0, qb, query_block, 0)


def _sb_call(q, kt, v, *, qb=SB_QBLOCKS_PER_STEP):
    B, L, D = q.shape
    nq = L // Q_BLOCK
    rows = qb * Q_BLOCK
    return pl.pallas_call(
        functools.partial(_sb_kernel, nk=nq, qb=qb),
        grid=(B, nq // qb),
        in_specs=[pl.BlockSpec((None, rows, D), lambda b, i: (b, i, 0)),
                  pl.BlockSpec((None, nq, D, Q_BLOCK), lambda b, i: (b, 0, 0, 0)),
                  pl.BlockSpec((None, L, D), lambda b, i: (b, 0, 0))],
        out_specs=pl.BlockSpec((None, rows, D), lambda b, i: (b, i, 0)),
        out_shape=jax.ShapeDtypeStruct((B, L, D), BF16),
        scratch_shapes=[pltpu.VMEM((SB_PAIRS, Q_BLOCK, SB_PAIR), F32),
                        pltpu.VMEM((SB_PAIRS, Q_BLOCK, 2 * Q_BLOCK), F32),
                        pltpu.VMEM((nq, SB_PAIRS, SB_PAIR, 2 * Q_BLOCK), BF16),
                        pltpu.VMEM((nq, SB_PAIRS, 2 * Q_BLOCK, SB_PAIR), BF16)],
        compiler_params=_params(2),
        name="sb_attn",
    )(q, kt, v)


def _sg_kernel(x_ref, g_ref, win_ref, vg_ref, ws_ref, bst_ref, m_ref, u_s, v_s, *, tl):
    xn = _rms(x_ref[...], g_ref[...]).astype(BF16)
    h = _gelu(jnp.dot(xn, win_ref[...], preferred_element_type=F32))
    u_s[...] = h[:, :D_MODEL]
    v_s[...] = _rms(h[:, D_MODEL:], vg_ref[...]).astype(BF16)
    r = lax.broadcasted_iota(jnp.int32, (CHUNK, CHUNK), 0)
    c = lax.broadcasted_iota(jnp.int32, (CHUNK, CHUNK), 1)
    for g in range(SG_GROUPS):
        cols = slice(g * SG_HEAD_DIM, (g + 1) * SG_HEAD_DIM)
        w = jnp.where(r >= c, ws_ref[g], 0.0).astype(BF16)
        bias = jnp.broadcast_to(bst_ref[:, g:g + 1], (CHUNK, SG_HEAD_DIM))
        for ch in range(tl // CHUNK):
            rows = slice(ch * CHUNK, (ch + 1) * CHUNK)
            sv = jnp.dot(w, v_s[rows, cols], preferred_element_type=F32) + bias
            m_ref[rows, cols] = (u_s[rows, cols] * sv).astype(BF16)


def _sg_call(x, g, win, vg, ws, bst, *, tl=SG_ROWS):
    B, L, D = x.shape
    row = lambda b, i: (b, i, 0)
    return pl.pallas_call(
        functools.partial(_sg_kernel, tl=tl),
        grid=(B, L // tl),
        in_specs=[pl.BlockSpec((None, tl, D), row), _resident(g.shape), _resident(win.shape),
                  _resident(vg.shape), _resident(ws.shape), _resident(bst.shape)],
        out_specs=pl.BlockSpec((None, tl, D), row),
        out_shape=jax.ShapeDtypeStruct((B, L, D), BF16),
        scratch_shapes=[pltpu.VMEM((tl, D), F32), pltpu.VMEM((tl, D), BF16)],
        compiler_params=_params(2),
        name="sg_gate",
    )(x, g, win, vg, ws, bst)


def _ssm_prep_kernel(lre_ref, lim_ref, ldt_ref, cre_ref, cim_ref, ar_ref, ai_ref, ctr_ref, cti_ref):
    lr = jnp.minimum(lre_ref[...], -1e-4)
    li = lim_ref[...]
    dt = jnp.exp(ldt_ref[...])
    mag = jnp.exp(dt * lr)
    ar = mag * jnp.cos(dt * li)
    ai = mag * jnp.sin(dt * li)
    den = lr * lr + li * li
    cr = ((ar - 1.0) * lr + ai * li) / den
    ci = (ai * lr - (ar - 1.0) * li) / den
    ar_ref[...] = ar
    ai_ref[...] = ai
    cre = cre_ref[...]
    cim = cim_ref[...]
    cr3 = cr[:, None, :]
    ci3 = ci[:, None, :]
    ctr_ref[...] = cre * cr3 - cim * ci3
    cti_ref[...] = cre * ci3 + cim * cr3


def _ssm_prep_call(lam_re, lam_im, log_dt, c_re, c_im):
    G, P = lam_re.shape
    H = c_re.shape[1]
    return pl.pallas_call(
        _ssm_prep_kernel,
        out_shape=[jax.ShapeDtypeStruct((G, P), F32), jax.ShapeDtypeStruct((G, P), F32),
                   jax.ShapeDtypeStruct((G, H, P), F32), jax.ShapeDtypeStruct((G, H, P), F32)],
        name="ssm_prep",
    )(lam_re, lam_im, log_dt.reshape(G, 1), c_re, c_im)


def _ssm_kernel(x_ref, g_ref, win_ref, wb_ref, wc_ref, ar_ref, ai_ref, d_ref, o_ref,
                u_lb, bu, y_lb, st, *, tl, nb):
    S = SSM_SLAB_STATES

    @pl.when(pl.program_id(0) == 0)
    def _():
        st[...] = jnp.zeros_like(st)

    for b in range(nb):
        xn = _rms(x_ref[b], g_ref[...]).astype(BF16)
        u = jnp.dot(xn, win_ref[...], preferred_element_type=F32)
        for j in range(SSM_SLABS):
            u_lb[j, pl.ds(b, tl, stride=nb), :] = u[:, j * LANES:(j + 1) * LANES]

    for j in range(SSM_SLABS):
        uj = u_lb[j]
        bu[...] = jnp.dot(uj.astype(BF16), wb_ref[j], preferred_element_type=F32)
        ar = jnp.broadcast_to(ar_ref[j:j + 1, :], (nb, S))
        ai = jnp.broadcast_to(ai_ref[j:j + 1, :], (nb, S))

        def step(t, carry):
            s_re, s_im = carry
            r0 = pl.multiple_of(t * nb, nb)
            n_re = ar * s_re - ai * s_im + bu[pl.ds(r0, nb), 0:S]
            n_im = ar * s_im + ai * s_re + bu[pl.ds(r0, nb), S:2 * S]
            bu[pl.ds(r0, nb), 0:S] = n_re
            bu[pl.ds(r0, nb), S:2 * S] = n_im
            return n_re, n_im

        s_re, s_im = lax.fori_loop(0, tl, step, (st[j, :, 0:S], st[j, :, S:2 * S]), unroll=True)
        st[j, :, 0:S] = s_re
        st[j, :, S:2 * S] = s_im
        y = jnp.dot(bu[...].astype(BF16), wc_ref[j], preferred_element_type=F32)
        y_lb[j] = _gelu(y + d_ref[j:j + 1, :] * uj)

    for b in range(nb):
        for j in range(SSM_SLABS):
            o_ref[b, :, j * LANES:(j + 1) * LANES] = y_lb[j, pl.ds(b, tl, stride=nb), :].astype(BF16)


def _ssm_call(x, g, win, wb, wc, ar, ai, d, *, tl=SSM_STEPS):
    B, L, D = x.shape
    R = B * tl
    blk = lambda i: (0, i, 0)
    return pl.pallas_call(
        functools.partial(_ssm_kernel, tl=tl, nb=B),
        grid=(L // tl,),
        in_specs=[pl.BlockSpec((B, tl, D), blk), _resident(g.shape), _resident(win.shape),
                  _resident(wb.shape), _resident(wc.shape), _resident(ar.shape), _resident(ai.shape),
                  _resident(d.shape)],
        out_specs=pl.BlockSpec((B, tl, D), blk),
        out_shape=jax.ShapeDtypeStruct((B, L, D), BF16),
        scratch_shapes=[pltpu.VMEM((SSM_SLABS, R, LANES), F32),
                        pltpu.VMEM((R, 2 * SSM_SLAB_STATES), F32),
                        pltpu.VMEM((SSM_SLABS, R, LANES), F32),
                        pltpu.VMEM((SSM_SLABS, B, 2 * SSM_SLAB_STATES), F32)],
        compiler_params=_params(1),
        name="ssm_scan",
    )(x, g, win, wb, wc, ar, ai, d)


def _ssm_block_weights(b_re, b_im, ct_re, ct_im):
    J, GL, P, H = SSM_SLABS, SSM_SLAB_GROUPS, SSM_STATE, SSM_GROUP
    eye = jnp.eye(GL, dtype=F32)

    def in_side(b):
        bt = b.reshape(J, GL, P, H).transpose(0, 1, 3, 2)
        return (bt[:, :, :, None, :] * eye[None, :, None, :, None]).reshape(J, GL * H, GL * P)

    def out_side(c):
        ct = c.reshape(J, GL, H, P).transpose(0, 1, 3, 2)
        return (ct[:, :, :, None, :] * eye[None, :, None, :, None]).reshape(J, GL * P, GL * H)

    wb = jnp.concatenate([in_side(b_re), in_side(b_im)], axis=2).astype(BF16)
    wc = jnp.concatenate([out_side(ct_re), -out_side(ct_im)], axis=1).astype(BF16)
    return wb, wc


def kernel(x, norm_g, final_norm_g, sb_w_qkv, sb_w_o, sg_w_in, sg_norm_g, sg_w_s, sg_b, sg_w_o,
           ssm_w_in, ssm_lam_re, ssm_lam_im, ssm_log_dt, ssm_b_re, ssm_b_im, ssm_c_re, ssm_c_im,
           ssm_d, ssm_w_glu, ffn_w_up, ffn_conv_w, ffn_conv_b, ffn_w_down):
    D = D_MODEL
    F = D_FF
    for i in range(DEPTH):
        mixer = i % N_MIXERS
        j = i // N_MIXERS
        g1 = norm_g[i, 0].reshape(1, D)
        g2 = norm_g[i, 1].reshape(1, D)
        if mixer == 0:
            w = sb_w_qkv[j]
            q, kt, v = _qkv_call(x, g1, w[:, :D].astype(BF16), w[:, D:2 * D].T.astype(BF16),
                                 w[:, 2 * D:].astype(BF16))
            m = _sb_call(q, kt, v)
            wm = sb_w_o[j].astype(BF16)
            glu = False
        elif mixer == 1:
            m = _sg_call(x, g1, sg_w_in[j].astype(BF16), sg_norm_g[j].reshape(1, D), sg_w_s[j],
                         sg_b[j].T)
            wm = sg_w_o[j].astype(BF16)
            glu = False
        else:
            ar, ai, ct_re, ct_im = _ssm_prep_call(ssm_lam_re[j], ssm_lam_im[j], ssm_log_dt[j],
                                                  ssm_c_re[j], ssm_c_im[j])
            wb, wc = _ssm_block_weights(ssm_b_re[j], ssm_b_im[j], ct_re, ct_im)
            m = _ssm_call(x, g1, ssm_w_in[j].astype(BF16), wb, wc,
                          ar.reshape(SSM_SLABS, SSM_SLAB_STATES), ai.reshape(SSM_SLABS, SSM_SLAB_STATES),
                          ssm_d[j].reshape(SSM_SLABS, LANES))
            wm = ssm_w_glu[j].astype(BF16)
            glu = True
        wu = ffn_w_up[i]
        cw = ffn_conv_w[i]
        cb = ffn_conv_b[i].reshape(1, 2 * F)
        fg = final_norm_g.reshape(1, D) if i == DEPTH - 1 else None
        x = _ffn_call(x, m, wm, g2, wu[:, :F].astype(BF16), wu[:, F:].astype(BF16),
                      cw[:, :F], cw[:, F:], cb[:, :F], cb[:, F:], ffn_w_down[i].astype(BF16), fg,
                      glu=glu)
    return x
```

```python
import functools
import math

import jax
import jax.numpy as jnp
from jax import lax
from jax.experimental import pallas as pl
from jax.experimental.pallas import tpu as pltpu

F32 = jnp.float32
BF16 = jnp.bfloat16

D_MODEL = 1024
EPS = 1e-6
LOG2E = math.log2(math.e)
SB_HEAD_DIM = 64
SB_PAIR = 2 * SB_HEAD_DIM
SB_PAIRS = D_MODEL // SB_PAIR
Q_BLOCK = 128
MASKED_LOGIT = -1e30
EXIT_LOGIT = -110.0
CHUNK = 128
SG_GROUPS = 8
SG_HEAD_DIM = D_MODEL // SG_GROUPS
SSM_GROUP = 16
SSM_GROUPS = 64
SSM_STATE = 64
SSM_SLABS = 8
SSM_SLAB_GROUPS = SSM_GROUPS // SSM_SLABS
SSM_SLAB_STATES = SSM_SLAB_GROUPS * SSM_STATE
D_FF = 2816
CONV_K = 3
DEPTH = 4
N_MIXERS = 3

V7X_VMEM_LIMIT_BYTES = 56 * 1024 * 1024
V7X_FFN_VMEM_LIMIT_BYTES = 60 * 1024 * 1024
SUBLANES = 8
LANES = 128

FFN_ROWS = 1024
FFN_FC = 256
QKV_ROWS = 1024
SB_QBLOCKS_PER_STEP = 8
SB_FUSED_BLOCKS = 3
SG_ROWS = 1024
SSM_STEPS = 128


def _rms(x, g):
    return x * lax.rsqrt(jnp.mean(x * x, axis=-1, keepdims=True) + EPS) * g


def _sigmoid(x):
    return 1.0 / (1.0 + jnp.exp(-x))


def _gelu(x):
    c = math.sqrt(2.0 / math.pi)
    return 0.5 * x * (1.0 + jnp.tanh(c * (x + 0.044715 * (x * x * x))))


def _resident(shape):
    nd = len(shape)
    return pl.BlockSpec(shape, lambda *_: (0,) * nd, pipeline_mode=pl.Buffered(1))


def _params(n_axes, vmem_limit_bytes=V7X_VMEM_LIMIT_BYTES):
    return pltpu.CompilerParams(
        dimension_semantics=("arbitrary",) * n_axes,
        vmem_limit_bytes=vmem_limit_bytes,
    )


def _ffn_kernel(*refs, glu, final, tl, fc, n_chunks):
    if final:
        (x_ref, m_ref, wm_ref, g_ref, wa_ref, wg_ref, cwa_ref, cwg_ref, cba_ref, cbg_ref, wd_ref,
         fg_ref, o_ref, xn_s, hsa, hsg, cara, carg, act) = refs
    else:
        (x_ref, m_ref, wm_ref, g_ref, wa_ref, wg_ref, cwa_ref, cwg_ref, cba_ref, cbg_ref, wd_ref,
         o_ref, xn_s, hsa, hsg, cara, carg, act) = refs
        fg_ref = None

    @pl.when(pl.program_id(1) == 0)
    def _():
        cara[...] = jnp.zeros_like(cara)
        carg[...] = jnp.zeros_like(carg)

    mm = jnp.dot(m_ref[...], wm_ref[...], preferred_element_type=F32)
    if glu:
        mix = mm[:, :D_MODEL] * _sigmoid(mm[:, D_MODEL:])
    else:
        mix = mm
    x1 = x_ref[...] + mix
    o_ref[...] = x1
    xn_s[...] = _rms(x1, g_ref[...]).astype(BF16)

    def conv(hs, car, cw_ref, cb_ref, h, cols):
        hs[0:SUBLANES, :] = car[:, cols]
        hs[SUBLANES:SUBLANES + tl, :] = h
        car[:, cols] = h[tl - SUBLANES:tl, :]
        w = cw_ref[:, cols]
        y = (w[2:3, :] * hs[SUBLANES:SUBLANES + tl, :]
             + w[1:2, :] * hs[SUBLANES - 1:SUBLANES - 1 + tl, :]
             + w[0:1, :] * hs[SUBLANES - 2:SUBLANES - 2 + tl, :])
        return y + cb_ref[:, cols]

    for c in range(n_chunks):
        cols = slice(c * fc, (c + 1) * fc)
        xn = xn_s[...]
        ha = jnp.dot(xn, wa_ref[:, cols], preferred_element_type=F32)
        hg = jnp.dot(xn, wg_ref[:, cols], preferred_element_type=F32)
        ya = conv(hsa, cara, cwa_ref, cba_ref, ha, cols)
        yg = conv(hsg, carg, cwg_ref, cbg_ref, hg, cols)
        act[:, cols] = (yg * _sigmoid(yg) * ya).astype(BF16)

    head = (n_chunks - 1) * fc
    out = (o_ref[...] + jnp.dot(act[:, :head], wd_ref[:head, :], preferred_element_type=F32)
           + jnp.dot(act[:, head:], wd_ref[head:, :], preferred_element_type=F32))
    if final:
        out = _rms(out, fg_ref[...])
    o_ref[...] = out


def _ffn_call(x, m, wm, g, wa, wg, cwa, cwg, cba, cbg, wd, fg, *, glu, tl=FFN_ROWS, fc=FFN_FC):
    B, L, D = x.shape
    F = wa.shape[1]
    dm = m.shape[2]
    final = fg is not None
    n_chunks = F // fc
    row = lambda b, i: (b, i, 0)
    in_specs = [
        pl.BlockSpec((None, tl, D), row),
        pl.BlockSpec((None, tl, dm), row),
        _resident(wm.shape), _resident(g.shape), _resident(wa.shape), _resident(wg.shape),
        _resident(cwa.shape), _resident(cwg.shape), _resident(cba.shape), _resident(cbg.shape),
        _resident(wd.shape),
    ]
    args = [x, m, wm, g, wa, wg, cwa, cwg, cba, cbg, wd]
    if final:
        in_specs.append(_resident(fg.shape))
        args.append(fg)
    return pl.pallas_call(
        functools.partial(_ffn_kernel, glu=glu, final=final, tl=tl, fc=fc, n_chunks=n_chunks),
        grid=(B, L // tl),
        in_specs=in_specs,
        out_specs=pl.BlockSpec((None, tl, D), row),
        out_shape=jax.ShapeDtypeStruct((B, L, D), F32),
        scratch_shapes=[
            pltpu.VMEM((tl, D), BF16),
            pltpu.VMEM((tl + SUBLANES, fc), F32),
            pltpu.VMEM((tl + SUBLANES, fc), F32),
            pltpu.VMEM((SUBLANES, F), F32),
            pltpu.VMEM((SUBLANES, F), F32),
            pltpu.VMEM((tl, F), BF16),
        ],
        compiler_params=_params(2, V7X_FFN_VMEM_LIMIT_BYTES),
        name="ffn",
    )(*args)


def _qkv_kernel(x_ref, g_ref, wq_ref, wkt_ref, wv_ref, q_ref, kt_ref, v_ref, *, tl):
    xn = _rms(x_ref[...], g_ref[...]).astype(BF16)
    q = jnp.dot(xn, wq_ref[...], preferred_element_type=F32) * (SB_HEAD_DIM ** -0.5)
    q_ref[...] = q.astype(BF16)
    v_ref[...] = jnp.dot(xn, wv_ref[...], preferred_element_type=F32).astype(BF16)
    kt = lax.dot_general(wkt_ref[...], xn, (((1,), (1,)), ((), ())), preferred_element_type=F32)
    kt = kt.astype(BF16)
    for t in range(tl // Q_BLOCK):
        kt_ref[t] = kt[:, t * Q_BLOCK:(t + 1) * Q_BLOCK]


def _qkv_call(x, g, wq, wkt, wv, *, tl=QKV_ROWS):
    B, L, D = x.shape
    nk = tl // Q_BLOCK
    row = lambda b, i: (b, i, 0)
    return pl.pallas_call(
        functools.partial(_qkv_kernel, tl=tl),
        grid=(B, L // tl),
        in_specs=[pl.BlockSpec((None, tl, D), row), _resident(g.shape), _resident(wq.shape),
                  _resident(wkt.shape), _resident(wv.shape)],
        out_specs=[pl.BlockSpec((None, tl, D), row),
                   pl.BlockSpec((None, nk, D, Q_BLOCK), lambda b, i: (b, i, 0, 0)),
                   pl.BlockSpec((None, tl, D), row)],
        out_shape=[jax.ShapeDtypeStruct((B, L, D), BF16),
                   jax.ShapeDtypeStruct((B, L // Q_BLOCK, D, Q_BLOCK), BF16),
                   jax.ShapeDtypeStruct((B, L, D), BF16)],
        compiler_params=_params(2),
        name="sb_qkv",
    )(x, g, wq, wkt, wv)


def _sb_kernel(q_ref, kt_ref, v_ref, o_ref, acc, csum, kbd, vbd, *, nk, qb):
    step = pl.program_id(1)
    W = 2 * Q_BLOCK
    pairs = range(SB_PAIRS)
    col = lambda hp: slice(hp * SB_PAIR, (hp + 1) * SB_PAIR)

    @pl.when(step == 0)
    def _():
        d_row = lax.broadcasted_iota(jnp.int32, (SB_PAIR, Q_BLOCK), 0)
        d_col = lax.broadcasted_iota(jnp.int32, (Q_BLOCK, SB_PAIR), 1)

        def build(jb, carry):
            k0 = pl.multiple_of(jb * Q_BLOCK, Q_BLOCK)
            for hp in pairs:
                kt = kt_ref[jb, col(hp), :]
                zk = jnp.zeros_like(kt)
                kbd[jb, hp] = jnp.concatenate([jnp.where(d_row < SB_HEAD_DIM, kt, zk),
                                               jnp.where(d_row >= SB_HEAD_DIM, kt, zk)], axis=1)
                v2 = v_ref[pl.ds(k0, Q_BLOCK), col(hp)]
                zv = jnp.zeros_like(v2)
                vbd[jb, hp] = jnp.concatenate([jnp.where(d_col < SB_HEAD_DIM, v2, zv),
                                               jnp.where(d_col >= SB_HEAD_DIM, v2, zv)], axis=0)
            return carry

        lax.fori_loop(0, nk, build, 0)

    r = lax.broadcasted_iota(jnp.int32, (W, W), 0)
    c = lax.broadcasted_iota(jnp.int32, (W, W), 1)
    same = (r < Q_BLOCK) == (c < Q_BLOCK)
    tri = jnp.where(same & (r > c), 1.0, 0.0).astype(BF16)
    ones = jnp.where(same, 1.0, 0.0).astype(BF16)
    tri_ones = jnp.concatenate([tri, ones], axis=1)

    t_row = lax.broadcasted_iota(jnp.int32, (Q_BLOCK, W), 0)
    s_col = lax.broadcasted_iota(jnp.int32, (Q_BLOCK, W), 1)
    s_col = jnp.where(s_col >= Q_BLOCK, s_col - Q_BLOCK, s_col)
    causal = s_col < t_row

    def query_block(qq, carry):
        i = step * qb + qq
        rows = pl.ds(pl.multiple_of(qq * Q_BLOCK, Q_BLOCK), Q_BLOCK)

        def scores(j, hp, masked):
            z = jnp.dot(q_ref[rows, col(hp)], kbd[j, hp], preferred_element_type=F32)
            ls = jnp.minimum(z, 0.0) - jnp.log(1.0 + jnp.exp2(jnp.abs(z) * -LOG2E))
            l1 = ls - z
            if masked:
                l1 = jnp.where(causal, l1, 0.0)
                ls = jnp.where(causal, ls, MASKED_LOGIT)
            return ls, l1.astype(BF16)

        def weights(hp, ls, l1, first):
            es = jnp.dot(l1, tri_ones, preferred_element_type=F32)
            if first:
                w = jnp.exp(ls + es[:, :W])
                later = es[:, W:]
            else:
                later = csum[hp]
                w = jnp.exp(ls + es[:, :W] + later)
                later = later + es[:, W:]
            csum[hp] = later
            return w.astype(BF16), later

        def tiles(js, first_masked):
            staged, wts = {}, {}
            top = [None]

            def stage_scores(k):
                staged[k] = [scores(js[k], hp, first_masked and k == 0) for hp in pairs]

            def stage_weights(k):
                top[0] = None
                wts[k] = []
                for hp in pairs:
                    w, later = weights(hp, *staged[k][hp], first_masked and k == 0)
                    top[0] = later if top[0] is None else jnp.maximum(top[0], later)
                    wts[k].append(w)

            def stage_values(k):
                for hp in pairs:
                    pv = jnp.dot(wts[k][hp], vbd[js[k], hp], preferred_element_type=F32)
                    if first_masked and k == 0:
                        acc[hp] = pv
                    else:
                        acc[hp] += pv

            for k in range(len(js) + 2):
                if k < len(js):
                    stage_scores(k)
                if 0 <= k - 1 < len(js):
                    stage_weights(k - 1)
                if 0 <= k - 2 < len(js):
                    stage_values(k - 2)
            return (jnp.max(top[0]) >= EXIT_LOGIT).astype(jnp.int32)

        def cond(c):
            j, live = c
            return jnp.logical_and(j >= 0, live > 0)

        def body(c):
            j, _ = c
            return j - 1, tiles([j], False)

        n0 = SB_FUSED_BLOCKS
        start = lax.cond(i >= n0 - 1,
                         lambda: (i - n0, tiles([i - k for k in range(n0)], True)),
                         lambda: (i - 1, tiles([i], True)))
        lax.while_loop(cond, body, start)

        for hp in pairs:
            o_ref[rows, col(hp)] = acc[hp].astype(BF16)
        return carry

    lax.fori_loop(0, qb, query_block, 0)


def _sb_call(q, kt, v, *, qb=SB_QBLOCKS_PER_STEP):
    B, L, D = q.shape
    nq = L // Q_BLOCK
    rows = qb * Q_BLOCK
    return pl.pallas_call(
        functools.partial(_sb_kernel, nk=nq, qb=qb),
        grid=(B, nq // qb),
        in_specs=[pl.BlockSpec((None, rows, D), lambda b, i: (b, i, 0)),
                  pl.BlockSpec((None, nq, D, Q_BLOCK), lambda b, i: (b, 0, 0, 0)),
                  pl.BlockSpec((None, L, D), lambda b, i: (b, 0, 0))],
        out_specs=pl.BlockSpec((None, rows, D), lambda b, i: (b, i, 0)),
        out_shape=jax.ShapeDtypeStruct((B, L, D), BF16),
        scratch_shapes=[pltpu.VMEM((SB_PAIRS, Q_BLOCK, SB_PAIR), F32),
                        pltpu.VMEM((SB_PAIRS, Q_BLOCK, 2 * Q_BLOCK), F32),
                        pltpu.VMEM((nq, SB_PAIRS, SB_PAIR, 2 * Q_BLOCK), BF16),
                        pltpu.VMEM((nq, SB_PAIRS, 2 * Q_BLOCK, SB_PAIR), BF16)],
        compiler_params=_params(2),
        name="sb_attn",
    )(q, kt, v)


def _sg_kernel(x_ref, g_ref, win_ref, vg_ref, ws_ref, bst_ref, m_ref, u_s, v_s, *, tl):
    xn = _rms(x_ref[...], g_ref[...]).astype(BF16)
    h = _gelu(jnp.dot(xn, win_ref[...], preferred_element_type=F32))
    u_s[...] = h[:, :D_MODEL]
    v_s[...] = _rms(h[:, D_MODEL:], vg_ref[...]).astype(BF16)
    r = lax.broadcasted_iota(jnp.int32, (CHUNK, CHUNK), 0)
    c = lax.broadcasted_iota(jnp.int32, (CHUNK, CHUNK), 1)
    for g in range(SG_GROUPS):
        cols = slice(g * SG_HEAD_DIM, (g + 1) * SG_HEAD_DIM)
        w = jnp.where(r >= c, ws_ref[g], 0.0).astype(BF16)
        bias = jnp.broadcast_to(bst_ref[:, g:g + 1], (CHUNK, SG_HEAD_DIM))
        for ch in range(tl // CHUNK):
            rows = slice(ch * CHUNK, (ch + 1) * CHUNK)
            sv = jnp.dot(w, v_s[rows, cols], preferred_element_type=F32) + bias
            m_ref[rows, cols] = (u_s[rows, cols] * sv).astype(BF16)


def _sg_call(x, g, win, vg, ws, bst, *, tl=SG_ROWS):
    B, L, D = x.shape
    row = lambda b, i: (b, i, 0)
    return pl.pallas_call(
        functools.partial(_sg_kernel, tl=tl),
        grid=(B, L // tl),
        in_specs=[pl.BlockSpec((None, tl, D), row), _resident(g.shape), _resident(win.shape),
                  _resident(vg.shape), _resident(ws.shape), _resident(bst.shape)],
        out_specs=pl.BlockSpec((None, tl, D), row),
        out_shape=jax.ShapeDtypeStruct((B, L, D), BF16),
        scratch_shapes=[pltpu.VMEM((tl, D), F32), pltpu.VMEM((tl, D), BF16)],
        compiler_params=_params(2),
        name="sg_gate",
    )(x, g, win, vg, ws, bst)


def _ssm_prep_kernel(lre_ref, lim_ref, ldt_ref, cre_ref, cim_ref, ar_ref, ai_ref, ctr_ref, cti_ref):
    lr = jnp.minimum(lre_ref[...], -1e-4)
    li = lim_ref[...]
    dt = jnp.exp(ldt_ref[...])
    mag = jnp.exp(dt * lr)
    ar = mag * jnp.cos(dt * li)
    ai = mag * jnp.sin(dt * li)
    den = lr * lr + li * li
    cr = ((ar - 1.0) * lr + ai * li) / den
    ci = (ai * lr - (ar - 1.0) * li) / den
    ar_ref[...] = ar
    ai_ref[...] = ai
    cre = cre_ref[...]
    cim = cim_ref[...]
    cr3 = cr[:, None, :]
    ci3 = ci[:, None, :]
    ctr_ref[...] = cre * cr3 - cim * ci3
    cti_ref[...] = cre * ci3 + cim * cr3


def _ssm_prep_call(lam_re, lam_im, log_dt, c_re, c_im):
    G, P = lam_re.shape
    H = c_re.shape[1]
    return pl.pallas_call(
        _ssm_prep_kernel,
        out_shape=[jax.ShapeDtypeStruct((G, P), F32), jax.ShapeDtypeStruct((G, P), F32),
                   jax.ShapeDtypeStruct((G, H, P), F32), jax.ShapeDtypeStruct((G, H, P), F32)],
        name="ssm_prep",
    )(lam_re, lam_im, log_dt.reshape(G, 1), c_re, c_im)


def _ssm_kernel(x_ref, g_ref, win_ref, wb_ref, wc_ref, ar_ref, ai_ref, d_ref, o_ref,
                u_lb, bu, y_lb, st, *, tl, nb):
    S = SSM_SLAB_STATES

    @pl.when(pl.program_id(0) == 0)
    def _():
        st[...] = jnp.zeros_like(st)

    for b in range(nb):
        xn = _rms(x_ref[b], g_ref[...]).astype(BF16)
        u = jnp.dot(xn, win_ref[...], preferred_element_type=F32)
        for j in range(SSM_SLABS):
            u_lb[j, pl.ds(b, tl, stride=nb), :] = u[:, j * LANES:(j + 1) * LANES]

    for j in range(SSM_SLABS):
        uj = u_lb[j]
        bu[...] = jnp.dot(uj.astype(BF16), wb_ref[j], preferred_element_type=F32)
        ar = jnp.broadcast_to(ar_ref[j:j + 1, :], (nb, S))
        ai = jnp.broadcast_to(ai_ref[j:j + 1, :], (nb, S))

        def step(t, carry):
            s_re, s_im = carry
            r0 = pl.multiple_of(t * nb, nb)
            n_re = ar * s_re - ai * s_im + bu[pl.ds(r0, nb), 0:S]
            n_im = ar * s_im + ai * s_re + bu[pl.ds(r0, nb), S:2 * S]
            bu[pl.ds(r0, nb), 0:S] = n_re
            bu[pl.ds(r0, nb), S:2 * S] = n_im
            return n_re, n_im

        s_re, s_im = lax.fori_loop(0, tl, step, (st[j, :, 0:S], st[j, :, S:2 * S]), unroll=True)
        st[j, :, 0:S] = s_re
        st[j, :, S:2 * S] = s_im
        y = jnp.dot(bu[...].astype(BF16), wc_ref[j], preferred_element_type=F32)
        y_lb[j] = _gelu(y + d_ref[j:j + 1, :] * uj)

    for b in range(nb):
        for j in range(SSM_SLABS):
            o_ref[b, :, j * LANES:(j + 1) * LANES] = y_lb[j, pl.ds(b, tl, stride=nb), :].astype(BF16)


def _ssm_call(x, g, win, wb, wc, ar, ai, d, *, tl=SSM_STEPS):
    B, L, D = x.shape
    R = B * tl
    blk = lambda i: (0, i, 0)
    return pl.pallas_call(
        functools.partial(_ssm_kernel, tl=tl, nb=B),
        grid=(L // tl,),
        in_specs=[pl.BlockSpec((B, tl, D), blk), _resident(g.shape), _resident(win.shape),
                  _resident(wb.shape), _resident(wc.shape), _resident(ar.shape), _resident(ai.shape),
                  _resident(d.shape)],
        out_specs=pl.BlockSpec((B, tl, D), blk),
        out_shape=jax.ShapeDtypeStruct((B, L, D), BF16),
        scratch_shapes=[pltpu.VMEM((SSM_SLABS, R, LANES), F32),
                        pltpu.VMEM((R, 2 * SSM_SLAB_STATES), F32),
                        pltpu.VMEM((SSM_SLABS, R, LANES), F32),
                        pltpu.VMEM((SSM_SLABS, B, 2 * SSM_SLAB_STATES), F32)],
        compiler_params=_params(1),
        name="ssm_scan",
    )(x, g, win, wb, wc, ar, ai, d)


def _ssm_block_weights(b_re, b_im, ct_re, ct_im):
    J, GL, P, H = SSM_SLABS, SSM_SLAB_GROUPS, SSM_STATE, SSM_GROUP
    eye = jnp.eye(GL, dtype=F32)

    def in_side(b):
        bt = b.reshape(J, GL, P, H).transpose(0, 1, 3, 2)
        return (bt[:, :, :, None, :] * eye[None, :, None, :, None]).reshape(J, GL * H, GL * P)

    def out_side(c):
        ct = c.reshape(J, GL, H, P).transpose(0, 1, 3, 2)
        return (ct[:, :, :, None, :] * eye[None, :, None, :, None]).reshape(J, GL * P, GL * H)

    wb = jnp.concatenate([in_side(b_re), in_side(b_im)], axis=2).astype(BF16)
    wc = jnp.concatenate([out_side(ct_re), -out_side(ct_im)], axis=1).astype(BF16)
    return wb, wc


def kernel(x, norm_g, final_norm_g, sb_w_qkv, sb_w_o, sg_w_in, sg_norm_g, sg_w_s, sg_b, sg_w_o,
           ssm_w_in, ssm_lam_re, ssm_lam_im, ssm_log_dt, ssm_b_re, ssm_b_im, ssm_c_re, ssm_c_im,
           ssm_d, ssm_w_glu, ffn_w_up, ffn_conv_w, ffn_conv_b, ffn_w_down):
    D = D_MODEL
    F = D_FF
    for i in range(DEPTH):
        mixer = i % N_MIXERS
        j = i // N_MIXERS
        g1 = norm_g[i, 0].reshape(1, D)
        g2 = norm_g[i, 1].reshape(1, D)
        if mixer == 0:
            w = sb_w_qkv[j]
            q, kt, v = _qkv_call(x, g1, w[:, :D].astype(BF16), w[:, D:2 * D].T.astype(BF16),
                                 w[:, 2 * D:].astype(BF16))
            m = _sb_call(q, kt, v)
            wm = sb_w_o[j].astype(BF16)
            glu = False
        elif mixer == 1:
            m = _sg_call(x, g1, sg_w_in[j].astype(BF16), sg_norm_g[j].reshape(1, D), sg_w_s[j],
                         sg_b[j].T)
            wm = sg_w_o[j].astype(BF16)
            glu = False
        else:
            ar, ai, ct_re, ct_im = _ssm_prep_call(ssm_lam_re[j], ssm_lam_im[j], ssm_log_dt[j],
                                                  ssm_c_re[j], ssm_c_im[j])
            wb, wc = _ssm_block_weights(ssm_b_re[j], ssm_b_im[j], ct_re, ct_im)
            m = _ssm_call(x, g1, ssm_w_in[j].astype(BF16), wb, wc,
                          ar.reshape(SSM_SLABS, SSM_SLAB_STATES), ai.reshape(SSM_SLABS, SSM_SLAB_STATES),
                          ssm_d[j].reshape(SSM_SLABS, LANES))
            wm = ssm_w_glu[j].astype(BF16)
            glu = True
        wu = ffn_w_up[i]
        cw = ffn_conv_w[i]
        cb = ffn_conv_b[i].reshape(1, 2 * F)
        fg = final_norm_g.reshape(1, D) if i == DEPTH - 1 else None
        x = _ffn_call(x, m, wm, g2, wu[:, :F].astype(BF16), wu[:, F:].astype(BF16),
                      cw[:, :F], cw[:, F:], cb[:, :F], cb[:, F:], ffn_w_down[i].astype(BF16), fg,
                      glu=glu)
    return x
```

```python
import functools
import math

import jax
import jax.numpy as jnp
from jax import lax
from jax.experimental import pallas as pl
from jax.experimental.pallas import tpu as pltpu

F32 = jnp.float32
BF16 = jnp.bfloat16

D_MODEL = 1024
EPS = 1e-6
LOG2E = math.log2(math.e)
SB_HEAD_DIM = 64
SB_PAIR = 2 * SB_HEAD_DIM
SB_PAIRS = D_MODEL // SB_PAIR
Q_BLOCK = 128
MASKED_LOGIT = -1e30
EXIT_LOGIT = -110.0
CHUNK = 128
SG_GROUPS = 8
SG_HEAD_DIM = D_MODEL // SG_GROUPS
SSM_GROUP = 16
SSM_GROUPS = 64
SSM_STATE = 64
SSM_SLABS = 8
SSM_SLAB_GROUPS = SSM_GROUPS // SSM_SLABS
SSM_SLAB_STATES = SSM_SLAB_GROUPS * SSM_STATE
D_FF = 2816
CONV_K = 3
DEPTH = 4
N_MIXERS = 3

V7X_VMEM_LIMIT_BYTES = 56 * 1024 * 1024
V7X_FFN_VMEM_LIMIT_BYTES = 60 * 1024 * 1024
SUBLANES = 8
LANES = 128

FFN_ROWS = 1024
FFN_FC = 256
QKV_ROWS = 1024
SB_QBLOCKS_PER_STEP = 8
SB_FUSED_BLOCKS = 3
SG_ROWS = 1024
SSM_STEPS = 128


def _rms(x, g):
    return x * lax.rsqrt(jnp.mean(x * x, axis=-1, keepdims=True) + EPS) * g


def _sigmoid(x):
    return 1.0 / (1.0 + jnp.exp(-x))


def _gelu(x):
    c = math.sqrt(2.0 / math.pi)
    return 0.5 * x * (1.0 + jnp.tanh(c * (x + 0.044715 * (x * x * x))))


def _resident(shape):
    nd = len(shape)
    return pl.BlockSpec(shape, lambda *_: (0,) * nd, pipeline_mode=pl.Buffered(1))


def _params(n_axes, vmem_limit_bytes=V7X_VMEM_LIMIT_BYTES):
    return pltpu.CompilerParams(
        dimension_semantics=("arbitrary",) * n_axes,
        vmem_limit_bytes=vmem_limit_bytes,
    )


def _ffn_kernel(*refs, glu, final, tl, fc, n_chunks):
    if final:
        (x_ref, m_ref, wm_ref, g_ref, wa_ref, wg_ref, cwa_ref, cwg_ref, cba_ref, cbg_ref, wd_ref,
         fg_ref, o_ref, xn_s, hsa, hsg, cara, carg, act) = refs
    else:
        (x_ref, m_ref, wm_ref, g_ref, wa_ref, wg_ref, cwa_ref, cwg_ref, cba_ref, cbg_ref, wd_ref,
         o_ref, xn_s, hsa, hsg, cara, carg, act) = refs
        fg_ref = None

    @pl.when(pl.program_id(1) == 0)
    def _():
        cara[...] = jnp.zeros_like(cara)
        carg[...] = jnp.zeros_like(carg)

    mm = jnp.dot(m_ref[...], wm_ref[...], preferred_element_type=F32)
    if glu:
        mix = mm[:, :D_MODEL] * _sigmoid(mm[:, D_MODEL:])
    else:
        mix = mm
    x1 = x_ref[...] + mix
    o_ref[...] = x1
    xn_s[...] = _rms(x1, g_ref[...]).astype(BF16)

    def conv(hs, car, cw_ref, cb_ref, h, cols):
        hs[0:SUBLANES, :] = car[:, cols]
        hs[SUBLANES:SUBLANES + tl, :] = h
        car[:, cols] = h[tl - SUBLANES:tl, :]
        w = cw_ref[:, cols]
        y = (w[2:3, :] * hs[SUBLANES:SUBLANES + tl, :]
             + w[1:2, :] * hs[SUBLANES - 1:SUBLANES - 1 + tl, :]
             + w[0:1, :] * hs[SUBLANES - 2:SUBLANES - 2 + tl, :])
        return y + cb_ref[:, cols]

    for c in range(n_chunks):
        cols = slice(c * fc, (c + 1) * fc)
        xn = xn_s[...]
        ha = jnp.dot(xn, wa_ref[:, cols], preferred_element_type=F32)
        hg = jnp.dot(xn, wg_ref[:, cols], preferred_element_type=F32)
        ya = conv(hsa, cara, cwa_ref, cba_ref, ha, cols)
        yg = conv(hsg, carg, cwg_ref, cbg_ref, hg, cols)
        act[:, cols] = (yg * _sigmoid(yg) * ya).astype(BF16)

    head = (n_chunks - 1) * fc
    out = (o_ref[...] + jnp.dot(act[:, :head], wd_ref[:head, :], preferred_element_type=F32)
           + jnp.dot(act[:, head:], wd_ref[head:, :], preferred_element_type=F32))
    if final:
        out = _rms(out, fg_ref[...])
    o_ref[...] = out


def _ffn_call(x, m, wm, g, wa, wg, cwa, cwg, cba, cbg, wd, fg, *, glu, tl=FFN_ROWS, fc=FFN_FC):
    B, L, D = x.shape
    F = wa.shape[1]
    dm = m.shape[2]
    final = fg is not None
    n_chunks = F // fc
    row = lambda b, i: (b, i, 0)
    in_specs = [
        pl.BlockSpec((None, tl, D), row),
        pl.BlockSpec((None, tl, dm), row),
        _resident(wm.shape), _resident(g.shape), _resident(wa.shape), _resident(wg.shape),
        _resident(cwa.shape), _resident(cwg.shape), _resident(cba.shape), _resident(cbg.shape),
        _resident(wd.shape),
    ]
    args = [x, m, wm, g, wa, wg, cwa, cwg, cba, cbg, wd]
    if final:
        in_specs.append(_resident(fg.shape))
        args.append(fg)
    return pl.pallas_call(
        functools.partial(_ffn_kernel, glu=glu, final=final, tl=tl, fc=fc, n_chunks=n_chunks),
        grid=(B, L // tl),
        in_specs=in_specs,
        out_specs=pl.BlockSpec((None, tl, D), row),
        out_shape=jax.ShapeDtypeStruct((B, L, D), F32),
        scratch_shapes=[
            pltpu.VMEM((tl, D), BF16),
            pltpu.VMEM((tl + SUBLANES, fc), F32),
            pltpu.VMEM((tl + SUBLANES, fc), F32),
            pltpu.VMEM((SUBLANES, F), F32),
            pltpu.VMEM((SUBLANES, F), F32),
            pltpu.VMEM((tl, F), BF16),
        ],
        compiler_params=_params(2, V7X_FFN_VMEM_LIMIT_BYTES),
        name="ffn",
    )(*args)


def _qkv_kernel(x_ref, g_ref, wq_ref, wkt_ref, wv_ref, q_ref, kt_ref, v_ref, *, tl):
    xn = _rms(x_ref[...], g_ref[...]).astype(BF16)
    q = jnp.dot(xn, wq_ref[...], preferred_element_type=F32) * (SB_HEAD_DIM ** -0.5)
    q_ref[...] = q.astype(BF16)
    v_ref[...] = jnp.dot(xn, wv_ref[...], preferred_element_type=F32).astype(BF16)
    kt = lax.dot_general(wkt_ref[...], xn, (((1,), (1,)), ((), ())), preferred_element_type=F32)
    kt = kt.astype(BF16)
    for t in range(tl // Q_BLOCK):
        kt_ref[t] = kt[:, t * Q_BLOCK:(t + 1) * Q_BLOCK]


def _qkv_call(x, g, wq, wkt, wv, *, tl=QKV_ROWS):
    B, L, D = x.shape
    nk = tl // Q_BLOCK
    row = lambda b, i: (b, i, 0)
    return pl.pallas_call(
        functools.partial(_qkv_kernel, tl=tl),
        grid=(B, L // tl),
        in_specs=[pl.BlockSpec((None, tl, D), row), _resident(g.shape), _resident(wq.shape),
                  _resident(wkt.shape), _resident(wv.shape)],
        out_specs=[pl.BlockSpec((None, tl, D), row),
                   pl.BlockSpec((None, nk, D, Q_BLOCK), lambda b, i: (b, i, 0, 0)),
                   pl.BlockSpec((None, tl, D), row)],
        out_shape=[jax.ShapeDtypeStruct((B, L, D), BF16),
                   jax.ShapeDtypeStruct((B, L // Q_BLOCK, D, Q_BLOCK), BF16),
                   jax.ShapeDtypeStruct((B, L, D), BF16)],
        compiler_params=_params(2),
        name="sb_qkv",
    )(x, g, wq, wkt, wv)


def _sb_kernel(q_ref, kt_ref, v_ref, o_ref, acc, csum, kbd, vbd, *, nk, qb):
    step = pl.program_id(1)
    W = 2 * Q_BLOCK
    pairs = range(SB_PAIRS)
    col = lambda hp: slice(hp * SB_PAIR, (hp + 1) * SB_PAIR)

    @pl.when(step == 0)
    def _():
        d_row = lax.broadcasted_iota(jnp.int32, (SB_PAIR, Q_BLOCK), 0)
        d_col = lax.broadcasted_iota(jnp.int32, (Q_BLOCK, SB_PAIR), 1)

        def build(jb, carry):
            k0 = pl.multiple_of(jb * Q_BLOCK, Q_BLOCK)
            for hp in pairs:
                kt = kt_ref[jb, col(hp), :]
                zk = jnp.zeros_like(kt)
                kbd[jb, hp] = jnp.concatenate([jnp.where(d_row < SB_HEAD_DIM, kt, zk),
                                               jnp.where(d_row >= SB_HEAD_DIM, kt, zk)], axis=1)
                v2 = v_ref[pl.ds(k0, Q_BLOCK), col(hp)]
                zv = jnp.zeros_like(v2)
                vbd[jb, hp] = jnp.concatenate([jnp.where(d_col < SB_HEAD_DIM, v2, zv),
                                               jnp.where(d_col >= SB_HEAD_DIM, v2, zv)], axis=0)
            return carry

        lax.fori_loop(0, nk, build, 0)

    r = lax.broadcasted_iota(jnp.int32, (W, W), 0)
    c = lax.broadcasted_iota(jnp.int32, (W, W), 1)
    same = (r < Q_BLOCK) == (c < Q_BLOCK)
    tri = jnp.where(same & (r > c), 1.0, 0.0).astype(BF16)
    ones = jnp.where(same, 1.0, 0.0).astype(BF16)
    tri_ones = jnp.concatenate([tri, ones], axis=1)

    t_row = lax.broadcasted_iota(jnp.int32, (Q_BLOCK, W), 0)
    s_col = lax.broadcasted_iota(jnp.int32, (Q_BLOCK, W), 1)
    s_col = jnp.where(s_col >= Q_BLOCK, s_col - Q_BLOCK, s_col)
    causal = s_col < t_row

    def query_block(qq, carry):
        i = step * qb + qq
        rows = pl.ds(pl.multiple_of(qq * Q_BLOCK, Q_BLOCK), Q_BLOCK)

        def scores(j, hp, masked):
            z = jnp.dot(q_ref[rows, col(hp)], kbd[j, hp], preferred_element_type=F32)
            ls = jnp.minimum(z, 0.0) - jnp.log(1.0 + jnp.exp2(jnp.abs(z) * -LOG2E))
            l1 = ls - z
            if masked:
                l1 = jnp.where(causal, l1, 0.0)
                ls = jnp.where(causal, ls, MASKED_LOGIT)
            return ls, l1.astype(BF16)

        def weights(hp, ls, l1, first):
            es = jnp.dot(l1, tri_ones, preferred_element_type=F32)
            if first:
                w = jnp.exp(ls + es[:, :W])
                later = es[:, W:]
            else:
                later = csum[hp]
                w = jnp.exp(ls + es[:, :W] + later)
                later = later + es[:, W:]
            csum[hp] = later
            return w.astype(BF16), later

        def tiles(js, first_masked):
            staged, wts = {}, {}
            top = [None]

            def stage_scores(k):
                staged[k] = [scores(js[k], hp, first_masked and k == 0) for hp in pairs]

            def stage_weights(k):
                top[0] = None
                wts[k] = []
                for hp in pairs:
                    w, later = weights(hp, *staged[k][hp], first_masked and k == 0)
                    top[0] = later if top[0] is None else jnp.maximum(top[0], later)
                    wts[k].append(w)

            def stage_values(k):
                for hp in pairs:
                    pv = jnp.dot(wts[k][hp], vbd[js[k], hp], preferred_element_type=F32)
                    if first_masked and k == 0:
                        acc[hp] = pv
                    else:
                        acc[hp] += pv

            for k in range(len(js) + 4):
                if k < len(js):
                    stage_scores(k)
                if 0 <= k - 2 < len(js):
                    stage_weights(k - 2)
                if 0 <= k - 4 < len(js):
                    stage_values(k - 4)
            return (jnp.max(top[0]) >= EXIT_LOGIT).astype(jnp.int32)

        def cond(c):
            j, live = c
            return jnp.logical_and(j >= 0, live > 0)

        def body(c):
            j, _ = c
            return j - 1, tiles([j], False)

        n0 = SB_FUSED_BLOCKS
        start = lax.cond(i >= n0 - 1,
                         lambda: (i - n0, tiles([i - k for k in range(n0)], True)),
                         lambda: (i - 1, tiles([i], True)))
        lax.while_loop(cond, body, start)

        for hp in pairs:
            o_ref[rows, col(hp)] = acc[hp].astype(BF16)
        return carry

    lax.fori_loop(0, qb, query_block, 0)


def _sb_call(q, kt, v, *, qb=SB_QBLOCKS_PER_STEP):
    B, L, D = q.shape
    nq = L // Q_BLOCK
    assert L % Q_BLOCK == 0 and nq % qb == 0, (L, qb)
    rows = qb * Q_BLOCK
    return pl.pallas_call(
        functools.partial(_sb_kernel, nk=nq, qb=qb),
        grid=(B, nq // qb),
        in_specs=[pl.BlockSpec((None, rows, D), lambda b, i: (b, i, 0)),
                  pl.BlockSpec((None, nq, D, Q_BLOCK), lambda b, i: (b, 0, 0, 0)),
                  pl.BlockSpec((None, L, D), lambda b, i: (b, 0, 0))],
        out_specs=pl.BlockSpec((None, rows, D), lambda b, i: (b, i, 0)),
        out_shape=jax.ShapeDtypeStruct((B, L, D), BF16),
        scratch_shapes=[pltpu.VMEM((SB_PAIRS, Q_BLOCK, SB_PAIR), F32),
                        pltpu.VMEM((SB_PAIRS, Q_BLOCK, 2 * Q_BLOCK), F32),
                        pltpu.VMEM((nq, SB_PAIRS, SB_PAIR, 2 * Q_BLOCK), BF16),
                        pltpu.VMEM((nq, SB_PAIRS, 2 * Q_BLOCK, SB_PAIR), BF16)],
        compiler_params=_params(2),
        name="sb_attn",
    )(q, kt, v)


def _sg_kernel(x_ref, g_ref, win_ref, vg_ref, ws_ref, bst_ref, m_ref, u_s, v_s, *, tl):
    xn = _rms(x_ref[...], g_ref[...]).astype(BF16)
    h = _gelu(jnp.dot(xn, win_ref[...], preferred_element_type=F32))
    u_s[...] = h[:, :D_MODEL]
    v_s[...] = _rms(h[:, D_MODEL:], vg_ref[...]).astype(BF16)
    r = lax.broadcasted_iota(jnp.int32, (CHUNK, CHUNK), 0)
    c = lax.broadcasted_iota(jnp.int32, (CHUNK, CHUNK), 1)
    for g in range(SG_GROUPS):
        cols = slice(g * SG_HEAD_DIM, (g + 1) * SG_HEAD_DIM)
        w = jnp.where(r >= c, ws_ref[g], 0.0).astype(BF16)
        bias = jnp.broadcast_to(bst_ref[:, g:g + 1], (CHUNK, SG_HEAD_DIM))
        for ch in range(tl // CHUNK):
            rows = slice(ch * CHUNK, (ch + 1) * CHUNK)
            sv = jnp.dot(w, v_s[rows, cols], preferred_element_type=F32) + bias
            m_ref[rows, cols] = (u_s[rows, cols] * sv).astype(BF16)


def _sg_call(x, g, win, vg, ws, bst, *, tl=SG_ROWS):
    B, L, D = x.shape
    row = lambda b, i: (b, i, 0)
    return pl.pallas_call(
        functools.partial(_sg_kernel, tl=tl),
        grid=(B, L // tl),
        in_specs=[pl.BlockSpec((None, tl, D), row), _resident(g.shape), _resident(win.shape),
                  _resident(vg.shape), _resident(ws.shape), _resident(bst.shape)],
        out_specs=pl.BlockSpec((None, tl, D), row),
        out_shape=jax.ShapeDtypeStruct((B, L, D), BF16),
        scratch_shapes=[pltpu.VMEM((tl, D), F32), pltpu.VMEM((tl, D), BF16)],
        compiler_params=_params(2),
        name="sg_gate",
    )(x, g, win, vg, ws, bst)


def _ssm_prep_kernel(lre_ref, lim_ref, ldt_ref, cre_ref, cim_ref, ar_ref, ai_ref, ctr_ref, cti_ref):
    lr = jnp.minimum(lre_ref[...], -1e-4)
    li = lim_ref[...]
    dt = jnp.exp(ldt_ref[...])
    mag = jnp.exp(dt * lr)
    ar = mag * jnp.cos(dt * li)
    ai = mag * jnp.sin(dt * li)
    den = lr * lr + li * li
    cr = ((ar - 1.0) * lr + ai * li) / den
    ci = (ai * lr - (ar - 1.0) * li) / den
    ar_ref[...] = ar
    ai_ref[...] = ai
    cre = cre_ref[...]
    cim = cim_ref[...]
    cr3 = cr[:, None, :]
    ci3 = ci[:, None, :]
    ctr_ref[...] = cre * cr3 - cim * ci3
    cti_ref[...] = cre * ci3 + cim * cr3


def _ssm_prep_call(lam_re, lam_im, log_dt, c_re, c_im):
    G, P = lam_re.shape
    H = c_re.shape[1]
    return pl.pallas_call(
        _ssm_prep_kernel,
        out_shape=[jax.ShapeDtypeStruct((G, P), F32), jax.ShapeDtypeStruct((G, P), F32),
                   jax.ShapeDtypeStruct((G, H, P), F32), jax.ShapeDtypeStruct((G, H, P), F32)],
        name="ssm_prep",
    )(lam_re, lam_im, log_dt.reshape(G, 1), c_re, c_im)


def _ssm_kernel(x_ref, g_ref, win_ref, wb_ref, wc_ref, ar_ref, ai_ref, d_ref, o_ref,
                u_lb, bu, y_lb, st, *, tl, nb):
    S = SSM_SLAB_STATES

    @pl.when(pl.program_id(0) == 0)
    def _():
        st[...] = jnp.zeros_like(st)

    for b in range(nb):
        xn = _rms(x_ref[b], g_ref[...]).astype(BF16)
        u = jnp.dot(xn, win_ref[...], preferred_element_type=F32)
        for j in range(SSM_SLABS):
            u_lb[j, pl.ds(b, tl, stride=nb), :] = u[:, j * LANES:(j + 1) * LANES]

    for j in range(SSM_SLABS):
        uj = u_lb[j]
        bu[...] = jnp.dot(uj.astype(BF16), wb_ref[j], preferred_element_type=F32)
        ar = jnp.broadcast_to(ar_ref[j:j + 1, :], (nb, S))
        ai = jnp.broadcast_to(ai_ref[j:j + 1, :], (nb, S))

        def step(t, carry):
            s_re, s_im = carry
            r0 = pl.multiple_of(t * nb, nb)
            n_re = ar * s_re - ai * s_im + bu[pl.ds(r0, nb), 0:S]
            n_im = ar * s_im + ai * s_re + bu[pl.ds(r0, nb), S:2 * S]
            bu[pl.ds(r0, nb), 0:S] = n_re
            bu[pl.ds(r0, nb), S:2 * S] = n_im
            return n_re, n_im

        s_re, s_im = lax.fori_loop(0, tl, step, (st[j, :, 0:S], st[j, :, S:2 * S]), unroll=True)
        st[j, :, 0:S] = s_re
        st[j, :, S:2 * S] = s_im
        y = jnp.dot(bu[...].astype(BF16), wc_ref[j], preferred_element_type=F32)
        y_lb[j] = _gelu(y + d_ref[j:j + 1, :] * uj)

    for b in range(nb):
        for j in range(SSM_SLABS):
            o_ref[b, :, j * LANES:(j + 1) * LANES] = y_lb[j, pl.ds(b, tl, stride=nb), :].astype(BF16)


def _ssm_call(x, g, win, wb, wc, ar, ai, d, *, tl=SSM_STEPS):
    B, L, D = x.shape
    R = B * tl
    blk = lambda i: (0, i, 0)
    return pl.pallas_call(
        functools.partial(_ssm_kernel, tl=tl, nb=B),
        grid=(L // tl,),
        in_specs=[pl.BlockSpec((B, tl, D), blk), _resident(g.shape), _resident(win.shape),
                  _resident(wb.shape), _resident(wc.shape), _resident(ar.shape), _resident(ai.shape),
                  _resident(d.shape)],
        out_specs=pl.BlockSpec((B, tl, D), blk),
        out_shape=jax.ShapeDtypeStruct((B, L, D), BF16),
        scratch_shapes=[pltpu.VMEM((SSM_SLABS, R, LANES), F32),
                        pltpu.VMEM((R, 2 * SSM_SLAB_STATES), F32),
                        pltpu.VMEM((SSM_SLABS, R, LANES), F32),
                        pltpu.VMEM((SSM_SLABS, B, 2 * SSM_SLAB_STATES), F32)],
        compiler_params=_params(1),
        name="ssm_scan",
    )(x, g, win, wb, wc, ar, ai, d)


def _ssm_block_weights(b_re, b_im, ct_re, ct_im):
    J, GL, P, H = SSM_SLABS, SSM_SLAB_GROUPS, SSM_STATE, SSM_GROUP
    eye = jnp.eye(GL, dtype=F32)

    def in_side(b):
        bt = b.reshape(J, GL, P, H).transpose(0, 1, 3, 2)
        return (bt[:, :, :, None, :] * eye[None, :, None, :, None]).reshape(J, GL * H, GL * P)

    def out_side(c):
        ct = c.reshape(J, GL, H, P).transpose(0, 1, 3, 2)
        return (ct[:, :, :, None, :] * eye[None, :, None, :, None]).reshape(J, GL * P, GL * H)

    wb = jnp.concatenate([in_side(b_re), in_side(b_im)], axis=2).astype(BF16)
    wc = jnp.concatenate([out_side(ct_re), -out_side(ct_im)], axis=1).astype(BF16)
    return wb, wc


def kernel(x, norm_g, final_norm_g, sb_w_qkv, sb_w_o, sg_w_in, sg_norm_g, sg_w_s, sg_b, sg_w_o,
           ssm_w_in, ssm_lam_re, ssm_lam_im, ssm_log_dt, ssm_b_re, ssm_b_im, ssm_c_re, ssm_c_im,
           ssm_d, ssm_w_glu, ffn_w_up, ffn_conv_w, ffn_conv_b, ffn_w_down):
    D = D_MODEL
    F = D_FF
    for i in range(DEPTH):
        mixer = i % N_MIXERS
        j = i // N_MIXERS
        g1 = norm_g[i, 0].reshape(1, D)
        g2 = norm_g[i, 1].reshape(1, D)
        if mixer == 0:
            w = sb_w_qkv[j]
            q, kt, v = _qkv_call(x, g1, w[:, :D].astype(BF16), w[:, D:2 * D].T.astype(BF16),
                                 w[:, 2 * D:].astype(BF16))
            m = _sb_call(q, kt, v)
            wm = sb_w_o[j].astype(BF16)
            glu = False
        elif mixer == 1:
            m = _sg_call(x, g1, sg_w_in[j].astype(BF16), sg_norm_g[j].reshape(1, D), sg_w_s[j],
                         sg_b[j].T)
            wm = sg_w_o[j].astype(BF16)
            glu = False
        else:
            ar, ai, ct_re, ct_im = _ssm_prep_call(ssm_lam_re[j], ssm_lam_im[j], ssm_log_dt[j],
                                                  ssm_c_re[j], ssm_c_im[j])
            wb, wc = _ssm_block_weights(ssm_b_re[j], ssm_b_im[j], ct_re, ct_im)
            m = _ssm_call(x, g1, ssm_w_in[j].astype(BF16), wb, wc,
                          ar.reshape(SSM_SLABS, SSM_SLAB_STATES), ai.reshape(SSM_SLABS, SSM_SLAB_STATES),
                          ssm_d[j].reshape(SSM_SLABS, LANES))
            wm = ssm_w_glu[j].astype(BF16)
            glu = True
        wu = ffn_w_up[i]
        cw = ffn_conv_w[i]
        cb = ffn_conv_b[i].reshape(1, 2 * F)
        fg = final_norm_g.reshape(1, D) if i == DEPTH - 1 else None
        x = _ffn_call(x, m, wm, g2, wu[:, :F].astype(BF16), wu[:, F:].astype(BF16),
                      cw[:, :F], cw[:, F:], cb[:, :F], cb[:, F:], ffn_w_down[i].astype(BF16), fg,
                      glu=glu)
    return x
```

```python
import functools
import math

import jax
import jax.numpy as jnp
from jax import lax
from jax.experimental import pallas as pl
from jax.experimental.pallas import tpu as pltpu

F32 = jnp.float32
BF16 = jnp.bfloat16

D_MODEL = 1024
EPS = 1e-6
LOG2E = math.log2(math.e)
SB_HEAD_DIM = 64
SB_PAIR = 2 * SB_HEAD_DIM
SB_PAIRS = D_MODEL // SB_PAIR
Q_BLOCK = 128
MASKED_LOGIT = -1e30
EXIT_LOGIT = -110.0
CHUNK = 128
SG_GROUPS = 8
SG_HEAD_DIM = D_MODEL // SG_GROUPS
SSM_GROUP = 16
SSM_GROUPS = 64
SSM_STATE = 64
SSM_SLABS = 8
SSM_SLAB_GROUPS = SSM_GROUPS // SSM_SLABS
SSM_SLAB_STATES = SSM_SLAB_GROUPS * SSM_STATE
D_FF = 2816
CONV_K = 3
DEPTH = 4
N_MIXERS = 3

V7X_VMEM_LIMIT_BYTES = 56 * 1024 * 1024
V7X_FFN_VMEM_LIMIT_BYTES = 60 * 1024 * 1024
SUBLANES = 8
LANES = 128

FFN_ROWS = 1024
FFN_FC = 256
QKV_ROWS = 1024
SB_QBLOCKS_PER_STEP = 8
SB_FUSED_BLOCKS = 3
SG_ROWS = 1024
SSM_STEPS = 128


def _rms(x, g):
    return x * lax.rsqrt(jnp.mean(x * x, axis=-1, keepdims=True) + EPS) * g


def _sigmoid(x):
    return 1.0 / (1.0 + jnp.exp(-x))


def _gelu(x):
    c = math.sqrt(2.0 / math.pi)
    return 0.5 * x * (1.0 + jnp.tanh(c * (x + 0.044715 * (x * x * x))))


def _resident(shape):
    nd = len(shape)
    return pl.BlockSpec(shape, lambda *_: (0,) * nd, pipeline_mode=pl.Buffered(1))


def _params(n_axes, vmem_limit_bytes=V7X_VMEM_LIMIT_BYTES):
    return pltpu.CompilerParams(
        dimension_semantics=("arbitrary",) * n_axes,
        vmem_limit_bytes=vmem_limit_bytes,
    )


def _ffn_kernel(*refs, glu, final, tl, fc, n_chunks):
    if final:
        (x_ref, m_ref, wm_ref, g_ref, wa_ref, wg_ref, cwa_ref, cwg_ref, cba_ref, cbg_ref, wd_ref,
         fg_ref, o_ref, xn_s, hsa, hsg, cara, carg, act) = refs
    else:
        (x_ref, m_ref, wm_ref, g_ref, wa_ref, wg_ref, cwa_ref, cwg_ref, cba_ref, cbg_ref, wd_ref,
         o_ref, xn_s, hsa, hsg, cara, carg, act) = refs
        fg_ref = None

    @pl.when(pl.program_id(1) == 0)
    def _():
        cara[...] = jnp.zeros_like(cara)
        carg[...] = jnp.zeros_like(carg)

    mm = jnp.dot(m_ref[...], wm_ref[...], preferred_element_type=F32)
    if glu:
        mix = mm[:, :D_MODEL] * _sigmoid(mm[:, D_MODEL:])
    else:
        mix = mm
    x1 = x_ref[...] + mix
    o_ref[...] = x1
    xn_s[...] = _rms(x1, g_ref[...]).astype(BF16)

    def conv(hs, car, cw_ref, cb_ref, h, cols):
        hs[0:SUBLANES, :] = car[:, cols]
        hs[SUBLANES:SUBLANES + tl, :] = h
        car[:, cols] = h[tl - SUBLANES:tl, :]
        w = cw_ref[:, cols]
        y = (w[2:3, :] * hs[SUBLANES:SUBLANES + tl, :]
             + w[1:2, :] * hs[SUBLANES - 1:SUBLANES - 1 + tl, :]
             + w[0:1, :] * hs[SUBLANES - 2:SUBLANES - 2 + tl, :])
        return y + cb_ref[:, cols]

    for c in range(n_chunks):
        cols = slice(c * fc, (c + 1) * fc)
        xn = xn_s[...]
        ha = jnp.dot(xn, wa_ref[:, cols], preferred_element_type=F32)
        hg = jnp.dot(xn, wg_ref[:, cols], preferred_element_type=F32)
        ya = conv(hsa, cara, cwa_ref, cba_ref, ha, cols)
        yg = conv(hsg, carg, cwg_ref, cbg_ref, hg, cols)
        act[:, cols] = (yg * _sigmoid(yg) * ya).astype(BF16)

    head = (n_chunks - 1) * fc
    out = (o_ref[...] + jnp.dot(act[:, :head], wd_ref[:head, :], preferred_element_type=F32)
           + jnp.dot(act[:, head:], wd_ref[head:, :], preferred_element_type=F32))
    if final:
        out = _rms(out, fg_ref[...])
    o_ref[...] = out


def _ffn_call(x, m, wm, g, wa, wg, cwa, cwg, cba, cbg, wd, fg, *, glu, tl=FFN_ROWS, fc=FFN_FC):
    B, L, D = x.shape
    F = wa.shape[1]
    dm = m.shape[2]
    final = fg is not None
    n_chunks = F // fc
    row = lambda b, i: (b, i, 0)
    in_specs = [
        pl.BlockSpec((None, tl, D), row),
        pl.BlockSpec((None, tl, dm), row),
        _resident(wm.shape), _resident(g.shape), _resident(wa.shape), _resident(wg.shape),
        _resident(cwa.shape), _resident(cwg.shape), _resident(cba.shape), _resident(cbg.shape),
        _resident(wd.shape),
    ]
    args = [x, m, wm, g, wa, wg, cwa, cwg, cba, cbg, wd]
    if final:
        in_specs.append(_resident(fg.shape))
        args.append(fg)
    return pl.pallas_call(
        functools.partial(_ffn_kernel, glu=glu, final=final, tl=tl, fc=fc, n_chunks=n_chunks),
        grid=(B, L // tl),
        in_specs=in_specs,
        out_specs=pl.BlockSpec((None, tl, D), row),
        out_shape=jax.ShapeDtypeStruct((B, L, D), F32),
        scratch_shapes=[
            pltpu.VMEM((tl, D), BF16),
            pltpu.VMEM((tl + SUBLANES, fc), F32),
            pltpu.VMEM((tl + SUBLANES, fc), F32),
            pltpu.VMEM((SUBLANES, F), F32),
            pltpu.VMEM((SUBLANES, F), F32),
            pltpu.VMEM((tl, F), BF16),
        ],
        compiler_params=_params(2, V7X_FFN_VMEM_LIMIT_BYTES),
        name="ffn",
    )(*args)


def _qkv_kernel(x_ref, g_ref, wq_ref, wkt_ref, wv_ref, q_ref, kt_ref, v_ref, *, tl):
    xn = _rms(x_ref[...], g_ref[...]).astype(BF16)
    q = jnp.dot(xn, wq_ref[...], preferred_element_type=F32) * (SB_HEAD_DIM ** -0.5)
    q_ref[...] = q.astype(BF16)
    v_ref[...] = jnp.dot(xn, wv_ref[...], preferred_element_type=F32).astype(BF16)
    kt = lax.dot_general(wkt_ref[...], xn, (((1,), (1,)), ((), ())), preferred_element_type=F32)
    kt = kt.astype(BF16)
    for t in range(tl // Q_BLOCK):
        kt_ref[t] = kt[:, t * Q_BLOCK:(t + 1) * Q_BLOCK]


def _qkv_call(x, g, wq, wkt, wv, *, tl=QKV_ROWS):
    B, L, D = x.shape
    nk = tl // Q_BLOCK
    row = lambda b, i: (b, i, 0)
    return pl.pallas_call(
        functools.partial(_qkv_kernel, tl=tl),
        grid=(B, L // tl),
        in_specs=[pl.BlockSpec((None, tl, D), row), _resident(g.shape), _resident(wq.shape),
                  _resident(wkt.shape), _resident(wv.shape)],
        out_specs=[pl.BlockSpec((None, tl, D), row),
                   pl.BlockSpec((None, nk, D, Q_BLOCK), lambda b, i: (b, i, 0, 0)),
                   pl.BlockSpec((None, tl, D), row)],
        out_shape=[jax.ShapeDtypeStruct((B, L, D), BF16),
                   jax.ShapeDtypeStruct((B, L // Q_BLOCK, D, Q_BLOCK), BF16),
                   jax.ShapeDtypeStruct((B, L, D), BF16)],
        compiler_params=_params(2),
        name="sb_qkv",
    )(x, g, wq, wkt, wv)


def _sb_kernel(q_ref, kt_ref, v_ref, o_ref, acc, csum, kbd, vbd, *, nk, qb):
    step = pl.program_id(1)
    W = 2 * Q_BLOCK
    pairs = range(SB_PAIRS)
    col = lambda hp: slice(hp * SB_PAIR, (hp + 1) * SB_PAIR)

    @pl.when(step == 0)
    def _():
        d_row = lax.broadcasted_iota(jnp.int32, (SB_PAIR, Q_BLOCK), 0)
        d_col = lax.broadcasted_iota(jnp.int32, (Q_BLOCK, SB_PAIR), 1)

        def build(jb, carry):
            k0 = pl.multiple_of(jb * Q_BLOCK, Q_BLOCK)
            for hp in pairs:
                kt = kt_ref[jb, col(hp), :]
                zk = jnp.zeros_like(kt)
                kbd[jb, hp] = jnp.concatenate([jnp.where(d_row < SB_HEAD_DIM, kt, zk),
                                               jnp.where(d_row >= SB_HEAD_DIM, kt, zk)], axis=1)
                v2 = v_ref[pl.ds(k0, Q_BLOCK), col(hp)]
                zv = jnp.zeros_like(v2)
                vbd[jb, hp] = jnp.concatenate([jnp.where(d_col < SB_HEAD_DIM, v2, zv),
                                               jnp.where(d_col >= SB_HEAD_DIM, v2, zv)], axis=0)
            return carry

        lax.fori_loop(0, nk, build, 0)

    r = lax.broadcasted_iota(jnp.int32, (W, W), 0)
    c = lax.broadcasted_iota(jnp.int32, (W, W), 1)
    same = (r < Q_BLOCK) == (c < Q_BLOCK)
    tri = jnp.where(same & (r > c), 1.0, 0.0).astype(BF16)
    ones = jnp.where(same, 1.0, 0.0).astype(BF16)
    tri_ones = jnp.concatenate([tri, ones], axis=1)

    t_row = lax.broadcasted_iota(jnp.int32, (Q_BLOCK, W), 0)
    s_col = lax.broadcasted_iota(jnp.int32, (Q_BLOCK, W), 1)
    s_col = jnp.where(s_col >= Q_BLOCK, s_col - Q_BLOCK, s_col)
    causal = s_col < t_row

    def query_pair(pp, carry):
        i0 = step * qb + 2 * pp
        row0 = [pl.multiple_of((2 * pp + n) * Q_BLOCK, Q_BLOCK) for n in range(2)]

        def scores(n, j, hp, masked):
            z = jnp.dot(q_ref[pl.ds(row0[n], Q_BLOCK), col(hp)], kbd[j, hp], preferred_element_type=F32)
            ls = jnp.minimum(z, 0.0) - jnp.log(1.0 + jnp.exp2(jnp.abs(z) * -LOG2E))
            l1 = ls - z
            if masked:
                l1 = jnp.where(causal, l1, 0.0)
                ls = jnp.where(causal, ls, MASKED_LOGIT)
            return ls, l1.astype(BF16)

        def weights(n, hp, ls, l1, first):
            es = jnp.dot(l1, tri_ones, preferred_element_type=F32)
            if first:
                w = jnp.exp(ls + es[:, :W])
                later = es[:, W:]
            else:
                later = csum[n, hp]
                w = jnp.exp(ls + es[:, :W] + later)
                later = later + es[:, W:]
            csum[n, hp] = later
            return w.astype(BF16), later

        def tiles(items, diagonal):
            staged, wts, top = {}, {}, {}

            def stage_scores(k):
                n, j = items[k]
                staged[k] = [scores(n, j, hp, diagonal[k]) for hp in pairs]

            def stage_weights(k):
                n, _ = items[k]
                wts[k] = []
                for hp in pairs:
                    w, later = weights(n, hp, *staged[k][hp], diagonal[k])
                    top[n] = later if hp == 0 else jnp.maximum(top[n], later)
                    wts[k].append(w)

            def stage_values(k):
                n, j = items[k]
                for hp in pairs:
                    pv = jnp.dot(wts[k][hp], vbd[j, hp], preferred_element_type=F32)
                    if diagonal[k]:
                        acc[n, hp] = pv
                    else:
                        acc[n, hp] += pv

            for k in range(len(items) + 2):
                if k < len(items):
                    stage_scores(k)
                if 0 <= k - 1 < len(items):
                    stage_weights(k - 1)
                if 0 <= k - 2 < len(items):
                    stage_values(k - 2)
            return {n: (jnp.max(t) >= EXIT_LOGIT).astype(jnp.int32) for n, t in top.items()}

        def head_pair():
            items = [(n, i0 + n - k) for k in range(SB_FUSED_BLOCKS) for n in range(2)]
            live = tiles(items, [k < 2 for k in range(len(items))])
            return live[0], live[1]

        def head_single():
            return tiles([(0, i0)], [True])[0], tiles([(1, i0 + 1)], [True])[1]

        fused = i0 >= SB_FUSED_BLOCKS - 1
        live = lax.cond(fused, head_pair, head_single)
        done = jnp.where(fused, SB_FUSED_BLOCKS, 1)

        for n in range(2):
            def cond(c):
                j, alive = c
                return jnp.logical_and(j >= 0, alive > 0)

            def body(c, n=n):
                j, _ = c
                return j - 1, tiles([(n, j)], [False])[n]

            lax.while_loop(cond, body, (i0 + n - done, live[n]))
            for hp in pairs:
                o_ref[pl.ds(row0[n], Q_BLOCK), col(hp)] = acc[n, hp].astype(BF16)
        return carry

    lax.fori_loop(0, qb // 2, query_pair, 0)


def _sb_call(q, kt, v, *, qb=SB_QBLOCKS_PER_STEP):
    B, L, D = q.shape
    nq = L // Q_BLOCK
    assert L % Q_BLOCK == 0 and qb % 2 == 0 and nq % qb == 0, (L, qb)
    rows = qb * Q_BLOCK
    return pl.pallas_call(
        functools.partial(_sb_kernel, nk=nq, qb=qb),
        grid=(B, nq // qb),
        in_specs=[pl.BlockSpec((None, rows, D), lambda b, i: (b, i, 0)),
                  pl.BlockSpec((None, nq, D, Q_BLOCK), lambda b, i: (b, 0, 0, 0)),
                  pl.BlockSpec((None, L, D), lambda b, i: (b, 0, 0))],
        out_specs=pl.BlockSpec((None, rows, D), lambda b, i: (b, i, 0)),
        out_shape=jax.ShapeDtypeStruct((B, L, D), BF16),
        scratch_shapes=[pltpu.VMEM((2, SB_PAIRS, Q_BLOCK, SB_PAIR), F32),
                        pltpu.VMEM((2, SB_PAIRS, Q_BLOCK, 2 * Q_BLOCK), F32),
                        pltpu.VMEM((nq, SB_PAIRS, SB_PAIR, 2 * Q_BLOCK), BF16),
                        pltpu.VMEM((nq, SB_PAIRS, 2 * Q_BLOCK, SB_PAIR), BF16)],
        compiler_params=_params(2),
        name="sb_attn",
    )(q, kt, v)


def _sg_kernel(x_ref, g_ref, win_ref, vg_ref, ws_ref, bst_ref, m_ref, u_s, v_s, *, tl):
    xn = _rms(x_ref[...], g_ref[...]).astype(BF16)
    h = _gelu(jnp.dot(xn, win_ref[...], preferred_element_type=F32))
    u_s[...] = h[:, :D_MODEL]
    v_s[...] = _rms(h[:, D_MODEL:], vg_ref[...]).astype(BF16)
    r = lax.broadcasted_iota(jnp.int32, (CHUNK, CHUNK), 0)
    c = lax.broadcasted_iota(jnp.int32, (CHUNK, CHUNK), 1)
    for g in range(SG_GROUPS):
        cols = slice(g * SG_HEAD_DIM, (g + 1) * SG_HEAD_DIM)
        w = jnp.where(r >= c, ws_ref[g], 0.0).astype(BF16)
        bias = jnp.broadcast_to(bst_ref[:, g:g + 1], (CHUNK, SG_HEAD_DIM))
        for ch in range(tl // CHUNK):
            rows = slice(ch * CHUNK, (ch + 1) * CHUNK)
            sv = jnp.dot(w, v_s[rows, cols], preferred_element_type=F32) + bias
            m_ref[rows, cols] = (u_s[rows, cols] * sv).astype(BF16)


def _sg_call(x, g, win, vg, ws, bst, *, tl=SG_ROWS):
    B, L, D = x.shape
    row = lambda b, i: (b, i, 0)
    return pl.pallas_call(
        functools.partial(_sg_kernel, tl=tl),
        grid=(B, L // tl),
        in_specs=[pl.BlockSpec((None, tl, D), row), _resident(g.shape), _resident(win.shape),
                  _resident(vg.shape), _resident(ws.shape), _resident(bst.shape)],
        out_specs=pl.BlockSpec((None, tl, D), row),
        out_shape=jax.ShapeDtypeStruct((B, L, D), BF16),
        scratch_shapes=[pltpu.VMEM((tl, D), F32), pltpu.VMEM((tl, D), BF16)],
        compiler_params=_params(2),
        name="sg_gate",
    )(x, g, win, vg, ws, bst)


def _ssm_prep_kernel(lre_ref, lim_ref, ldt_ref, cre_ref, cim_ref, ar_ref, ai_ref, ctr_ref, cti_ref):
    lr = jnp.minimum(lre_ref[...], -1e-4)
    li = lim_ref[...]
    dt = jnp.exp(ldt_ref[...])
    mag = jnp.exp(dt * lr)
    ar = mag * jnp.cos(dt * li)
    ai = mag * jnp.sin(dt * li)
    den = lr * lr + li * li
    cr = ((ar - 1.0) * lr + ai * li) / den
    ci = (ai * lr - (ar - 1.0) * li) / den
    ar_ref[...] = ar
    ai_ref[...] = ai
    cre = cre_ref[...]
    cim = cim_ref[...]
    cr3 = cr[:, None, :]
    ci3 = ci[:, None, :]
    ctr_ref[...] = cre * cr3 - cim * ci3
    cti_ref[...] = cre * ci3 + cim * cr3


def _ssm_prep_call(lam_re, lam_im, log_dt, c_re, c_im):
    G, P = lam_re.shape
    H = c_re.shape[1]
    return pl.pallas_call(
        _ssm_prep_kernel,
        out_shape=[jax.ShapeDtypeStruct((G, P), F32), jax.ShapeDtypeStruct((G, P), F32),
                   jax.ShapeDtypeStruct((G, H, P), F32), jax.ShapeDtypeStruct((G, H, P), F32)],
        name="ssm_prep",
    )(lam_re, lam_im, log_dt.reshape(G, 1), c_re, c_im)


def _ssm_kernel(x_ref, g_ref, win_ref, wb_ref, wc_ref, ar_ref, ai_ref, d_ref, o_ref,
                u_lb, bu, y_lb, st, *, tl, nb):
    S = SSM_SLAB_STATES

    @pl.when(pl.program_id(0) == 0)
    def _():
        st[...] = jnp.zeros_like(st)

    for b in range(nb):
        xn = _rms(x_ref[b], g_ref[...]).astype(BF16)
        u = jnp.dot(xn, win_ref[...], preferred_element_type=F32)
        for j in range(SSM_SLABS):
            u_lb[j, pl.ds(b, tl, stride=nb), :] = u[:, j * LANES:(j + 1) * LANES]

    for j in range(SSM_SLABS):
        uj = u_lb[j]
        bu[...] = jnp.dot(uj.astype(BF16), wb_ref[j], preferred_element_type=F32)
        ar = jnp.broadcast_to(ar_ref[j:j + 1, :], (nb, S))
        ai = jnp.broadcast_to(ai_ref[j:j + 1, :], (nb, S))

        def step(t, carry):
            s_re, s_im = carry
            r0 = pl.multiple_of(t * nb, nb)
            n_re = ar * s_re - ai * s_im + bu[pl.ds(r0, nb), 0:S]
            n_im = ar * s_im + ai * s_re + bu[pl.ds(r0, nb), S:2 * S]
            bu[pl.ds(r0, nb), 0:S] = n_re
            bu[pl.ds(r0, nb), S:2 * S] = n_im
            return n_re, n_im

        s_re, s_im = lax.fori_loop(0, tl, step, (st[j, :, 0:S], st[j, :, S:2 * S]), unroll=True)
        st[j, :, 0:S] = s_re
        st[j, :, S:2 * S] = s_im
        y = jnp.dot(bu[...].astype(BF16), wc_ref[j], preferred_element_type=F32)
        y_lb[j] = _gelu(y + d_ref[j:j + 1, :] * uj)

    for b in range(nb):
        for j in range(SSM_SLABS):
            o_ref[b, :, j * LANES:(j + 1) * LANES] = y_lb[j, pl.ds(b, tl, stride=nb), :].astype(BF16)


def _ssm_call(x, g, win, wb, wc, ar, ai, d, *, tl=SSM_STEPS):
    B, L, D = x.shape
    R = B * tl
    blk = lambda i: (0, i, 0)
    return pl.pallas_call(
        functools.partial(_ssm_kernel, tl=tl, nb=B),
        grid=(L // tl,),
        in_specs=[pl.BlockSpec((B, tl, D), blk), _resident(g.shape), _resident(win.shape),
                  _resident(wb.shape), _resident(wc.shape), _resident(ar.shape), _resident(ai.shape),
                  _resident(d.shape)],
        out_specs=pl.BlockSpec((B, tl, D), blk),
        out_shape=jax.ShapeDtypeStruct((B, L, D), BF16),
        scratch_shapes=[pltpu.VMEM((SSM_SLABS, R, LANES), F32),
                        pltpu.VMEM((R, 2 * SSM_SLAB_STATES), F32),
                        pltpu.VMEM((SSM_SLABS, R, LANES), F32),
                        pltpu.VMEM((SSM_SLABS, B, 2 * SSM_SLAB_STATES), F32)],
        compiler_params=_params(1),
        name="ssm_scan",
    )(x, g, win, wb, wc, ar, ai, d)


def _ssm_block_weights(b_re, b_im, ct_re, ct_im):
    J, GL, P, H = SSM_SLABS, SSM_SLAB_GROUPS, SSM_STATE, SSM_GROUP
    eye = jnp.eye(GL, dtype=F32)

    def in_side(b):
        bt = b.reshape(J, GL, P, H).transpose(0, 1, 3, 2)
        return (bt[:, :, :, None, :] * eye[None, :, None, :, None]).reshape(J, GL * H, GL * P)

    def out_side(c):
        ct = c.reshape(J, GL, H, P).transpose(0, 1, 3, 2)
        return (ct[:, :, :, None, :] * eye[None, :, None, :, None]).reshape(J, GL * P, GL * H)

    wb = jnp.concatenate([in_side(b_re), in_side(b_im)], axis=2).astype(BF16)
    wc = jnp.concatenate([out_side(ct_re), -out_side(ct_im)], axis=1).astype(BF16)
    return wb, wc


def kernel(x, norm_g, final_norm_g, sb_w_qkv, sb_w_o, sg_w_in, sg_norm_g, sg_w_s, sg_b, sg_w_o,
           ssm_w_in, ssm_lam_re, ssm_lam_im, ssm_log_dt, ssm_b_re, ssm_b_im, ssm_c_re, ssm_c_im,
           ssm_d, ssm_w_glu, ffn_w_up, ffn_conv_w, ffn_conv_b, ffn_w_down):
    D = D_MODEL
    F = D_FF
    for i in range(DEPTH):
        mixer = i % N_MIXERS
        j = i // N_MIXERS
        g1 = norm_g[i, 0].reshape(1, D)
        g2 = norm_g[i, 1].reshape(1, D)
        if mixer == 0:
            w = sb_w_qkv[j]
            q, kt, v = _qkv_call(x, g1, w[:, :D].astype(BF16), w[:, D:2 * D].T.astype(BF16),
                                 w[:, 2 * D:].astype(BF16))
            m = _sb_call(q, kt, v)
            wm = sb_w_o[j].astype(BF16)
            glu = False
        elif mixer == 1:
            m = _sg_call(x, g1, sg_w_in[j].astype(BF16), sg_norm_g[j].reshape(1, D), sg_w_s[j],
                         sg_b[j].T)
            wm = sg_w_o[j].astype(BF16)
            glu = False
        else:
            ar, ai, ct_re, ct_im = _ssm_prep_call(ssm_lam_re[j], ssm_lam_im[j], ssm_log_dt[j],
                                                  ssm_c_re[j], ssm_c_im[j])
            wb, wc = _ssm_block_weights(ssm_b_re[j], ssm_b_im[j], ct_re, ct_im)
            m = _ssm_call(x, g1, ssm_w_in[j].astype(BF16), wb, wc,
                          ar.reshape(SSM_SLABS, SSM_SLAB_STATES), ai.reshape(SSM_SLABS, SSM_SLAB_STATES),
                          ssm_d[j].reshape(SSM_SLABS, LANES))
            wm = ssm_w_glu[j].astype(BF16)
            glu = True
        wu = ffn_w_up[i]
        cw = ffn_conv_w[i]
        cb = ffn_conv_b[i].reshape(1, 2 * F)
        fg = final_norm_g.reshape(1, D) if i == DEPTH - 1 else None
        x = _ffn_call(x, m, wm, g2, wu[:, :F].astype(BF16), wu[:, F:].astype(BF16),
                      cw[:, :F], cw[:, F:], cb[:, :F], cb[:, F:], ffn_w_down[i].astype(BF16), fg,
                      glu=glu)
    return x
```

```python
import functools
import math

import jax
import jax.numpy as jnp
from jax import lax
from jax.experimental import pallas as pl
from jax.experimental.pallas import tpu as pltpu

F32 = jnp.float32
BF16 = jnp.bfloat16

D_MODEL = 1024
EPS = 1e-6
LOG2E = math.log2(math.e)
SB_HEAD_DIM = 64
SB_PAIR = 2 * SB_HEAD_DIM
SB_PAIRS = D_MODEL // SB_PAIR
Q_BLOCK = 128
MASKED_LOGIT = -1e30
EXIT_LOGIT = -110.0
CHUNK = 128
SG_GROUPS = 8
SG_HEAD_DIM = D_MODEL // SG_GROUPS
SSM_GROUP = 16
SSM_GROUPS = 64
SSM_STATE = 64
SSM_SLABS = 8
SSM_SLAB_GROUPS = SSM_GROUPS // SSM_SLABS
SSM_SLAB_STATES = SSM_SLAB_GROUPS * SSM_STATE
D_FF = 2816
CONV_K = 3
DEPTH = 4
N_MIXERS = 3

V7X_VMEM_LIMIT_BYTES = 56 * 1024 * 1024
V7X_FFN_VMEM_LIMIT_BYTES = 60 * 1024 * 1024
SUBLANES = 8
LANES = 128

FFN_ROWS = 1024
FFN_FC = 256
QKV_ROWS = 1024
SB_QBLOCKS_PER_STEP = 8
SB_FUSED_BLOCKS = 3
SB_QGROUP = 4
SG_ROWS = 1024
SSM_STEPS = 128


def _rms(x, g):
    return x * lax.rsqrt(jnp.mean(x * x, axis=-1, keepdims=True) + EPS) * g


def _sigmoid(x):
    return 1.0 / (1.0 + jnp.exp(-x))


def _gelu(x):
    c = math.sqrt(2.0 / math.pi)
    return 0.5 * x * (1.0 + jnp.tanh(c * (x + 0.044715 * (x * x * x))))


def _resident(shape):
    nd = len(shape)
    return pl.BlockSpec(shape, lambda *_: (0,) * nd, pipeline_mode=pl.Buffered(1))


def _params(n_axes, vmem_limit_bytes=V7X_VMEM_LIMIT_BYTES):
    return pltpu.CompilerParams(
        dimension_semantics=("arbitrary",) * n_axes,
        vmem_limit_bytes=vmem_limit_bytes,
    )


def _ffn_kernel(*refs, glu, final, tl, fc, n_chunks):
    if final:
        (x_ref, m_ref, wm_ref, g_ref, wa_ref, wg_ref, cwa_ref, cwg_ref, cba_ref, cbg_ref, wd_ref,
         fg_ref, o_ref, xn_s, hsa, hsg, cara, carg, act) = refs
    else:
        (x_ref, m_ref, wm_ref, g_ref, wa_ref, wg_ref, cwa_ref, cwg_ref, cba_ref, cbg_ref, wd_ref,
         o_ref, xn_s, hsa, hsg, cara, carg, act) = refs
        fg_ref = None

    @pl.when(pl.program_id(1) == 0)
    def _():
        cara[...] = jnp.zeros_like(cara)
        carg[...] = jnp.zeros_like(carg)

    mm = jnp.dot(m_ref[...], wm_ref[...], preferred_element_type=F32)
    if glu:
        mix = mm[:, :D_MODEL] * _sigmoid(mm[:, D_MODEL:])
    else:
        mix = mm
    x1 = x_ref[...] + mix
    o_ref[...] = x1
    xn_s[...] = _rms(x1, g_ref[...]).astype(BF16)

    def conv(hs, car, cw_ref, cb_ref, h, cols):
        hs[0:SUBLANES, :] = car[:, cols]
        hs[SUBLANES:SUBLANES + tl, :] = h
        car[:, cols] = h[tl - SUBLANES:tl, :]
        w = cw_ref[:, cols]
        y = (w[2:3, :] * hs[SUBLANES:SUBLANES + tl, :]
             + w[1:2, :] * hs[SUBLANES - 1:SUBLANES - 1 + tl, :]
             + w[0:1, :] * hs[SUBLANES - 2:SUBLANES - 2 + tl, :])
        return y + cb_ref[:, cols]

    for c in range(n_chunks):
        cols = slice(c * fc, (c + 1) * fc)
        xn = xn_s[...]
        ha = jnp.dot(xn, wa_ref[:, cols], preferred_element_type=F32)
        hg = jnp.dot(xn, wg_ref[:, cols], preferred_element_type=F32)
        ya = conv(hsa, cara, cwa_ref, cba_ref, ha, cols)
        yg = conv(hsg, carg, cwg_ref, cbg_ref, hg, cols)
        act[:, cols] = (yg * _sigmoid(yg) * ya).astype(BF16)

    head = (n_chunks - 1) * fc
    out = (o_ref[...] + jnp.dot(act[:, :head], wd_ref[:head, :], preferred_element_type=F32)
           + jnp.dot(act[:, head:], wd_ref[head:, :], preferred_element_type=F32))
    if final:
        out = _rms(out, fg_ref[...])
    o_ref[...] = out


def _ffn_call(x, m, wm, g, wa, wg, cwa, cwg, cba, cbg, wd, fg, *, glu, tl=FFN_ROWS, fc=FFN_FC):
    B, L, D = x.shape
    F = wa.shape[1]
    dm = m.shape[2]
    final = fg is not None
    n_chunks = F // fc
    row = lambda b, i: (b, i, 0)
    in_specs = [
        pl.BlockSpec((None, tl, D), row),
        pl.BlockSpec((None, tl, dm), row),
        _resident(wm.shape), _resident(g.shape), _resident(wa.shape), _resident(wg.shape),
        _resident(cwa.shape), _resident(cwg.shape), _resident(cba.shape), _resident(cbg.shape),
        _resident(wd.shape),
    ]
    args = [x, m, wm, g, wa, wg, cwa, cwg, cba, cbg, wd]
    if final:
        in_specs.append(_resident(fg.shape))
        args.append(fg)
    return pl.pallas_call(
        functools.partial(_ffn_kernel, glu=glu, final=final, tl=tl, fc=fc, n_chunks=n_chunks),
        grid=(B, L // tl),
        in_specs=in_specs,
        out_specs=pl.BlockSpec((None, tl, D), row),
        out_shape=jax.ShapeDtypeStruct((B, L, D), F32),
        scratch_shapes=[
            pltpu.VMEM((tl, D), BF16),
            pltpu.VMEM((tl + SUBLANES, fc), F32),
            pltpu.VMEM((tl + SUBLANES, fc), F32),
            pltpu.VMEM((SUBLANES, F), F32),
            pltpu.VMEM((SUBLANES, F), F32),
            pltpu.VMEM((tl, F), BF16),
        ],
        compiler_params=_params(2, V7X_FFN_VMEM_LIMIT_BYTES),
        name="ffn",
    )(*args)


def _qkv_kernel(x_ref, g_ref, wq_ref, wkt_ref, wv_ref, q_ref, kt_ref, v_ref, *, tl):
    xn = _rms(x_ref[...], g_ref[...]).astype(BF16)
    q = jnp.dot(xn, wq_ref[...], preferred_element_type=F32) * (SB_HEAD_DIM ** -0.5)
    q_ref[...] = q.astype(BF16)
    v_ref[...] = jnp.dot(xn, wv_ref[...], preferred_element_type=F32).astype(BF16)
    kt = lax.dot_general(wkt_ref[...], xn, (((1,), (1,)), ((), ())), preferred_element_type=F32)
    kt = kt.astype(BF16)
    for t in range(tl // Q_BLOCK):
        kt_ref[t] = kt[:, t * Q_BLOCK:(t + 1) * Q_BLOCK]


def _qkv_call(x, g, wq, wkt, wv, *, tl=QKV_ROWS):
    B, L, D = x.shape
    nk = tl // Q_BLOCK
    row = lambda b, i: (b, i, 0)
    return pl.pallas_call(
        functools.partial(_qkv_kernel, tl=tl),
        grid=(B, L // tl),
        in_specs=[pl.BlockSpec((None, tl, D), row), _resident(g.shape), _resident(wq.shape),
                  _resident(wkt.shape), _resident(wv.shape)],
        out_specs=[pl.BlockSpec((None, tl, D), row),
                   pl.BlockSpec((None, nk, D, Q_BLOCK), lambda b, i: (b, i, 0, 0)),
                   pl.BlockSpec((None, tl, D), row)],
        out_shape=[jax.ShapeDtypeStruct((B, L, D), BF16),
                   jax.ShapeDtypeStruct((B, L // Q_BLOCK, D, Q_BLOCK), BF16),
                   jax.ShapeDtypeStruct((B, L, D), BF16)],
        compiler_params=_params(2),
        name="sb_qkv",
    )(x, g, wq, wkt, wv)


def _sb_kernel(q_ref, kt_ref, v_ref, o_ref, acc, csum, kbd, vbd, *, nk, qb):
    step = pl.program_id(1)
    W = 2 * Q_BLOCK
    pairs = range(SB_PAIRS)
    col = lambda hp: slice(hp * SB_PAIR, (hp + 1) * SB_PAIR)

    @pl.when(step == 0)
    def _():
        d_row = lax.broadcasted_iota(jnp.int32, (SB_PAIR, Q_BLOCK), 0)
        d_col = lax.broadcasted_iota(jnp.int32, (Q_BLOCK, SB_PAIR), 1)

        def build(jb, carry):
            k0 = pl.multiple_of(jb * Q_BLOCK, Q_BLOCK)
            for hp in pairs:
                kt = kt_ref[jb, col(hp), :]
                zk = jnp.zeros_like(kt)
                kbd[jb, hp] = jnp.concatenate([jnp.where(d_row < SB_HEAD_DIM, kt, zk),
                                               jnp.where(d_row >= SB_HEAD_DIM, kt, zk)], axis=1)
                v2 = v_ref[pl.ds(k0, Q_BLOCK), col(hp)]
                zv = jnp.zeros_like(v2)
                vbd[jb, hp] = jnp.concatenate([jnp.where(d_col < SB_HEAD_DIM, v2, zv),
                                               jnp.where(d_col >= SB_HEAD_DIM, v2, zv)], axis=0)
            return carry

        lax.fori_loop(0, nk, build, 0)

    r = lax.broadcasted_iota(jnp.int32, (W, W), 0)
    c = lax.broadcasted_iota(jnp.int32, (W, W), 1)
    same = (r < Q_BLOCK) == (c < Q_BLOCK)
    tri = jnp.where(same & (r > c), 1.0, 0.0).astype(BF16)
    ones = jnp.where(same, 1.0, 0.0).astype(BF16)
    tri_ones = jnp.concatenate([tri, ones], axis=1)

    t_row = lax.broadcasted_iota(jnp.int32, (Q_BLOCK, W), 0)
    s_col = lax.broadcasted_iota(jnp.int32, (Q_BLOCK, W), 1)
    s_col = jnp.where(s_col >= Q_BLOCK, s_col - Q_BLOCK, s_col)
    causal = s_col < t_row

    def query_pair(pp, carry):
        i0 = step * qb + SB_QGROUP * pp
        row0 = [pl.multiple_of((SB_QGROUP * pp + n) * Q_BLOCK, Q_BLOCK) for n in range(SB_QGROUP)]

        def scores(n, j, hp, masked):
            z = jnp.dot(q_ref[pl.ds(row0[n], Q_BLOCK), col(hp)], kbd[j, hp], preferred_element_type=F32)
            ls = jnp.minimum(z, 0.0) - jnp.log(1.0 + jnp.exp2(jnp.abs(z) * -LOG2E))
            l1 = ls - z
            if masked:
                l1 = jnp.where(causal, l1, 0.0)
                ls = jnp.where(causal, ls, MASKED_LOGIT)
            return ls, l1.astype(BF16)

        def weights(n, hp, ls, l1, first):
            es = jnp.dot(l1, tri_ones, preferred_element_type=F32)
            if first:
                w = jnp.exp(ls + es[:, :W])
                later = es[:, W:]
            else:
                later = csum[n, hp]
                w = jnp.exp(ls + es[:, :W] + later)
                later = later + es[:, W:]
            csum[n, hp] = later
            return w.astype(BF16), later

        def tiles(items, diagonal):
            staged, wts, top = {}, {}, {}

            def stage_scores(k):
                n, j = items[k]
                staged[k] = [scores(n, j, hp, diagonal[k]) for hp in pairs]

            def stage_weights(k):
                n, _ = items[k]
                wts[k] = []
                for hp in pairs:
                    w, later = weights(n, hp, *staged[k][hp], diagonal[k])
                    top[n] = later if hp == 0 else jnp.maximum(top[n], later)
                    wts[k].append(w)

            def stage_values(k):
                n, j = items[k]
                for hp in pairs:
                    pv = jnp.dot(wts[k][hp], vbd[j, hp], preferred_element_type=F32)
                    if diagonal[k]:
                        acc[n, hp] = pv
                    else:
                        acc[n, hp] += pv

            for k in range(len(items) + 2):
                if k < len(items):
                    stage_scores(k)
                if 0 <= k - 1 < len(items):
                    stage_weights(k - 1)
                if 0 <= k - 2 < len(items):
                    stage_values(k - 2)
            return {n: (jnp.max(t) >= EXIT_LOGIT).astype(jnp.int32) for n, t in top.items()}

        def head_pair():
            items = [(n, i0 + n - k) for k in range(SB_FUSED_BLOCKS) for n in range(SB_QGROUP)]
            live = tiles(items, [k < SB_QGROUP for k in range(len(items))])
            return tuple(live[n] for n in range(SB_QGROUP))

        def head_single():
            return tuple(tiles([(n, i0 + n)], [True])[n] for n in range(SB_QGROUP))

        fused = i0 >= SB_FUSED_BLOCKS - 1
        live = lax.cond(fused, head_pair, head_single)
        done = jnp.where(fused, SB_FUSED_BLOCKS, 1)

        for n in range(SB_QGROUP):
            def cond(c):
                j, alive = c
                return jnp.logical_and(j >= 0, alive > 0)

            def body(c, n=n):
                j, _ = c
                return j - 1, tiles([(n, j)], [False])[n]

            lax.while_loop(cond, body, (i0 + n - done, live[n]))
            for hp in pairs:
                o_ref[pl.ds(row0[n], Q_BLOCK), col(hp)] = acc[n, hp].astype(BF16)
        return carry

    lax.fori_loop(0, qb // SB_QGROUP, query_pair, 0)


def _sb_call(q, kt, v, *, qb=SB_QBLOCKS_PER_STEP):
    B, L, D = q.shape
    nq = L // Q_BLOCK
    assert L % Q_BLOCK == 0 and qb % SB_QGROUP == 0 and nq % qb == 0, (L, qb)
    rows = qb * Q_BLOCK
    return pl.pallas_call(
        functools.partial(_sb_kernel, nk=nq, qb=qb),
        grid=(B, nq // qb),
        in_specs=[pl.BlockSpec((None, rows, D), lambda b, i: (b, i, 0)),
                  pl.BlockSpec((None, nq, D, Q_BLOCK), lambda b, i: (b, 0, 0, 0)),
                  pl.BlockSpec((None, L, D), lambda b, i: (b, 0, 0))],
        out_specs=pl.BlockSpec((None, rows, D), lambda b, i: (b, i, 0)),
        out_shape=jax.ShapeDtypeStruct((B, L, D), BF16),
        scratch_shapes=[pltpu.VMEM((SB_QGROUP, SB_PAIRS, Q_BLOCK, SB_PAIR), F32),
                        pltpu.VMEM((SB_QGROUP, SB_PAIRS, Q_BLOCK, 2 * Q_BLOCK), F32),
                        pltpu.VMEM((nq, SB_PAIRS, SB_PAIR, 2 * Q_BLOCK), BF16),
                        pltpu.VMEM((nq, SB_PAIRS, 2 * Q_BLOCK, SB_PAIR), BF16)],
        compiler_params=_params(2),
        name="sb_attn",
    )(q, kt, v)


def _sg_kernel(x_ref, g_ref, win_ref, vg_ref, ws_ref, bst_ref, m_ref, u_s, v_s, *, tl):
    xn = _rms(x_ref[...], g_ref[...]).astype(BF16)
    h = _gelu(jnp.dot(xn, win_ref[...], preferred_element_type=F32))
    u_s[...] = h[:, :D_MODEL]
    v_s[...] = _rms(h[:, D_MODEL:], vg_ref[...]).astype(BF16)
    r = lax.broadcasted_iota(jnp.int32, (CHUNK, CHUNK), 0)
    c = lax.broadcasted_iota(jnp.int32, (CHUNK, CHUNK), 1)
    for g in range(SG_GROUPS):
        cols = slice(g * SG_HEAD_DIM, (g + 1) * SG_HEAD_DIM)
        w = jnp.where(r >= c, ws_ref[g], 0.0).astype(BF16)
        bias = jnp.broadcast_to(bst_ref[:, g:g + 1], (CHUNK, SG_HEAD_DIM))
        for ch in range(tl // CHUNK):
            rows = slice(ch * CHUNK, (ch + 1) * CHUNK)
            sv = jnp.dot(w, v_s[rows, cols], preferred_element_type=F32) + bias
            m_ref[rows, cols] = (u_s[rows, cols] * sv).astype(BF16)


def _sg_call(x, g, win, vg, ws, bst, *, tl=SG_ROWS):
    B, L, D = x.shape
    row = lambda b, i: (b, i, 0)
    return pl.pallas_call(
        functools.partial(_sg_kernel, tl=tl),
        grid=(B, L // tl),
        in_specs=[pl.BlockSpec((None, tl, D), row), _resident(g.shape), _resident(win.shape),
                  _resident(vg.shape), _resident(ws.shape), _resident(bst.shape)],
        out_specs=pl.BlockSpec((None, tl, D), row),
        out_shape=jax.ShapeDtypeStruct((B, L, D), BF16),
        scratch_shapes=[pltpu.VMEM((tl, D), F32), pltpu.VMEM((tl, D), BF16)],
        compiler_params=_params(2),
        name="sg_gate",
    )(x, g, win, vg, ws, bst)


def _ssm_prep_kernel(lre_ref, lim_ref, ldt_ref, cre_ref, cim_ref, ar_ref, ai_ref, ctr_ref, cti_ref):
    lr = jnp.minimum(lre_ref[...], -1e-4)
    li = lim_ref[...]
    dt = jnp.exp(ldt_ref[...])
    mag = jnp.exp(dt * lr)
    ar = mag * jnp.cos(dt * li)
    ai = mag * jnp.sin(dt * li)
    den = lr * lr + li * li
    cr = ((ar - 1.0) * lr + ai * li) / den
    ci = (ai * lr - (ar - 1.0) * li) / den
    ar_ref[...] = ar
    ai_ref[...] = ai
    cre = cre_ref[...]
    cim = cim_ref[...]
    cr3 = cr[:, None, :]
    ci3 = ci[:, None, :]
    ctr_ref[...] = cre * cr3 - cim * ci3
    cti_ref[...] = cre * ci3 + cim * cr3


def _ssm_prep_call(lam_re, lam_im, log_dt, c_re, c_im):
    G, P = lam_re.shape
    H = c_re.shape[1]
    return pl.pallas_call(
        _ssm_prep_kernel,
        out_shape=[jax.ShapeDtypeStruct((G, P), F32), jax.ShapeDtypeStruct((G, P), F32),
                   jax.ShapeDtypeStruct((G, H, P), F32), jax.ShapeDtypeStruct((G, H, P), F32)],
        name="ssm_prep",
    )(lam_re, lam_im, log_dt.reshape(G, 1), c_re, c_im)


def _ssm_kernel(x_ref, g_ref, win_ref, wb_ref, wc_ref, ar_ref, ai_ref, d_ref, o_ref,
                u_lb, bu, y_lb, st, *, tl, nb):
    S = SSM_SLAB_STATES

    @pl.when(pl.program_id(0) == 0)
    def _():
        st[...] = jnp.zeros_like(st)

    for b in range(nb):
        xn = _rms(x_ref[b], g_ref[...]).astype(BF16)
        u = jnp.dot(xn, win_ref[...], preferred_element_type=F32)
        for j in range(SSM_SLABS):
            u_lb[j, pl.ds(b, tl, stride=nb), :] = u[:, j * LANES:(j + 1) * LANES]

    for j in range(SSM_SLABS):
        uj = u_lb[j]
        bu[...] = jnp.dot(uj.astype(BF16), wb_ref[j], preferred_element_type=F32)
        ar = jnp.broadcast_to(ar_ref[j:j + 1, :], (nb, S))
        ai = jnp.broadcast_to(ai_ref[j:j + 1, :], (nb, S))

        def step(t, carry):
            s_re, s_im = carry
            r0 = pl.multiple_of(t * nb, nb)
            n_re = ar * s_re - ai * s_im + bu[pl.ds(r0, nb), 0:S]
            n_im = ar * s_im + ai * s_re + bu[pl.ds(r0, nb), S:2 * S]
            bu[pl.ds(r0, nb), 0:S] = n_re
            bu[pl.ds(r0, nb), S:2 * S] = n_im
            return n_re, n_im

        s_re, s_im = lax.fori_loop(0, tl, step, (st[j, :, 0:S], st[j, :, S:2 * S]), unroll=True)
        st[j, :, 0:S] = s_re
        st[j, :, S:2 * S] = s_im
        y = jnp.dot(bu[...].astype(BF16), wc_ref[j], preferred_element_type=F32)
        y_lb[j] = _gelu(y + d_ref[j:j + 1, :] * uj)

    for b in range(nb):
        for j in range(SSM_SLABS):
            o_ref[b, :, j * LANES:(j + 1) * LANES] = y_lb[j, pl.ds(b, tl, stride=nb), :].astype(BF16)


def _ssm_call(x, g, win, wb, wc, ar, ai, d, *, tl=SSM_STEPS):
    B, L, D = x.shape
    R = B * tl
    blk = lambda i: (0, i, 0)
    return pl.pallas_call(
        functools.partial(_ssm_kernel, tl=tl, nb=B),
        grid=(L // tl,),
        in_specs=[pl.BlockSpec((B, tl, D), blk), _resident(g.shape), _resident(win.shape),
                  _resident(wb.shape), _resident(wc.shape), _resident(ar.shape), _resident(ai.shape),
                  _resident(d.shape)],
        out_specs=pl.BlockSpec((B, tl, D), blk),
        out_shape=jax.ShapeDtypeStruct((B, L, D), BF16),
        scratch_shapes=[pltpu.VMEM((SSM_SLABS, R, LANES), F32),
                        pltpu.VMEM((R, 2 * SSM_SLAB_STATES), F32),
                        pltpu.VMEM((SSM_SLABS, R, LANES), F32),
                        pltpu.VMEM((SSM_SLABS, B, 2 * SSM_SLAB_STATES), F32)],
        compiler_params=_params(1),
        name="ssm_scan",
    )(x, g, win, wb, wc, ar, ai, d)


def _ssm_block_weights(b_re, b_im, ct_re, ct_im):
    J, GL, P, H = SSM_SLABS, SSM_SLAB_GROUPS, SSM_STATE, SSM_GROUP
    eye = jnp.eye(GL, dtype=F32)

    def in_side(b):
        bt = b.reshape(J, GL, P, H).transpose(0, 1, 3, 2)
        return (bt[:, :, :, None, :] * eye[None, :, None, :, None]).reshape(J, GL * H, GL * P)

    def out_side(c):
        ct = c.reshape(J, GL, H, P).transpose(0, 1, 3, 2)
        return (ct[:, :, :, None, :] * eye[None, :, None, :, None]).reshape(J, GL * P, GL * H)

    wb = jnp.concatenate([in_side(b_re), in_side(b_im)], axis=2).astype(BF16)
    wc = jnp.concatenate([out_side(ct_re), -out_side(ct_im)], axis=1).astype(BF16)
    return wb, wc


def kernel(x, norm_g, final_norm_g, sb_w_qkv, sb_w_o, sg_w_in, sg_norm_g, sg_w_s, sg_b, sg_w_o,
           ssm_w_in, ssm_lam_re, ssm_lam_im, ssm_log_dt, ssm_b_re, ssm_b_im, ssm_c_re, ssm_c_im,
           ssm_d, ssm_w_glu, ffn_w_up, ffn_conv_w, ffn_conv_b, ffn_w_down):
    D = D_MODEL
    F = D_FF
    for i in range(DEPTH):
        mixer = i % N_MIXERS
        j = i // N_MIXERS
        g1 = norm_g[i, 0].reshape(1, D)
        g2 = norm_g[i, 1].reshape(1, D)
        if mixer == 0:
            w = sb_w_qkv[j]
            q, kt, v = _qkv_call(x, g1, w[:, :D].astype(BF16), w[:, D:2 * D].T.astype(BF16),
                                 w[:, 2 * D:].astype(BF16))
            m = _sb_call(q, kt, v)
            wm = sb_w_o[j].astype(BF16)
            glu = False
        elif mixer == 1:
            m = _sg_call(x, g1, sg_w_in[j].astype(BF16), sg_norm_g[j].reshape(1, D), sg_w_s[j],
                         sg_b[j].T)
            wm = sg_w_o[j].astype(BF16)
            glu = False
        else:
            ar, ai, ct_re, ct_im = _ssm_prep_call(ssm_lam_re[j], ssm_lam_im[j], ssm_log_dt[j],
                                                  ssm_c_re[j], ssm_c_im[j])
            wb, wc = _ssm_block_weights(ssm_b_re[j], ssm_b_im[j], ct_re, ct_im)
            m = _ssm_call(x, g1, ssm_w_in[j].astype(BF16), wb, wc,
                          ar.reshape(SSM_SLABS, SSM_SLAB_STATES), ai.reshape(SSM_SLABS, SSM_SLAB_STATES),
                          ssm_d[j].reshape(SSM_SLABS, LANES))
            wm = ssm_w_glu[j].astype(BF16)
            glu = True
        wu = ffn_w_up[i]
        cw = ffn_conv_w[i]
        cb = ffn_conv_b[i].reshape(1, 2 * F)
        fg = final_norm_g.reshape(1, D) if i == DEPTH - 1 else None
        x = _ffn_call(x, m, wm, g2, wu[:, :F].astype(BF16), wu[:, F:].astype(BF16),
                      cw[:, :F], cw[:, F:], cb[:, :F], cb[:, F:], ffn_w_down[i].astype(BF16), fg,
                      glu=glu)
    return x
```
